```python
import math
import jax, jax.numpy as jnp
from jax import lax
import numpy as np

D_MODEL = 4096
BATCH = 4
SEQ = 4096
DEPTH = 1

MIX_WIDTH = D_MODEL
REC_WIDTH = MIX_WIDTH // 2
ATTN_WIDTH = MIX_WIDTH - REC_WIDTH
HEAD_DIM = 128
N_ATTN_HEADS = ATTN_WIDTH // HEAD_DIM
N_REC_HEADS = 16
REC_HEAD_DIM = REC_WIDTH // N_REC_HEADS
CONV_WIDTH = 4
LRU_C = 8.0
DILATED_PATTERNS = ((128, 1), (512, 4), (2048, 16))
BLOCK = 128
REL_BUCKETS = 32
REL_MAX_DISTANCE = 2048
D_FF = ((8 * D_MODEL + 3 * 256 - 1) // (3 * 256)) * 256
N_ADA = 6
EPS = 1e-6
NEG_INF = -1e30

kernel_name = "hybrid_rglru_dilated_attn_block"


def rms_norm(x, g):
    x32 = x.astype(jnp.float32)
    y = x32 * lax.rsqrt(jnp.mean(x32 * x32, axis=-1, keepdims=True) + EPS) * g.astype(jnp.float32)
    return y.astype(x.dtype)


def t5_bucket(n):
    max_exact = REL_BUCKETS // 2
    nf = jnp.maximum(n, 1).astype(jnp.float32)
    large = max_exact + (jnp.log(nf / max_exact) / math.log(REL_MAX_DISTANCE / max_exact)
                         * (REL_BUCKETS - max_exact)).astype(jnp.int32)
    large = jnp.minimum(large, REL_BUCKETS - 1)
    return jnp.where(n < max_exact, n, large)


def causal_conv(x, w, b):
    C = x.shape[-1]
    y = lax.conv_general_dilated(x, w[:, None, :].astype(x.dtype), window_strides=(1,),
                                 padding=((CONV_WIDTH - 1, 0),),
                                 dimension_numbers=('NWC', 'WIO', 'NWC'),
                                 feature_group_count=C)
    return y + b.astype(x.dtype)


def rg_lru(xr, w_a, b_a, w_i, b_i, lam):
    B, S, R = xr.shape
    x32 = xr.astype(jnp.float32)
    xh = x32.reshape(B, S, N_REC_HEADS, REC_HEAD_DIM)
    r = jax.nn.sigmoid(jnp.einsum('bshi,hij->bshj', xh, w_a.astype(jnp.float32)).reshape(B, S, R)
                       + b_a.astype(jnp.float32))
    i = jax.nn.sigmoid(jnp.einsum('bshi,hij->bshj', xh, w_i.astype(jnp.float32)).reshape(B, S, R)
                       + b_i.astype(jnp.float32))
    log_a = -LRU_C * r * jax.nn.softplus(-lam.astype(jnp.float32))
    a = jnp.exp(log_a)
    u = jnp.sqrt(-jnp.expm1(2.0 * log_a)) * (i * x32)

    def combine(left, right):
        a1, b1 = left
        a2, b2 = right
        return a1 * a2, a2 * b1 + b2

    _, h = lax.associative_scan(combine, (a, u), axis=1)
    return h


def dilated_branch(q, k, v, rel_bias, window, dil):
    B, S, H, Dh = q.shape
    span = BLOCK * dil
    S_pad = -(-S // span) * span
    pad = S_pad - S
    L = S_pad // dil
    nb = L // BLOCK

    def split(t):
        t = jnp.pad(t, ((0, 0), (0, pad), (0, 0), (0, 0)))
        t = t.reshape(B, L, dil, H, Dh).transpose(0, 2, 1, 3, 4)
        return t.reshape(B, dil, nb, BLOCK, H, Dh)

    def with_prev(t):
        prev = jnp.concatenate([jnp.zeros_like(t[:, :, :1]), t[:, :, :-1]], axis=2)
        return jnp.concatenate([prev, t], axis=3)

    qb = split(q)
    kc = with_prev(split(k))
    vc = with_prev(split(v))

    s = jnp.einsum('brnqhd,brnkhd->brnhqk', qb, kc) * (HEAD_DIM ** -0.5)
    qi = jnp.arange(BLOCK, dtype=jnp.int32)[:, None]
    kj = jnp.arange(2 * BLOCK, dtype=jnp.int32)[None, :]
    dist = qi + BLOCK - kj
    band = (dist >= 0) & (dist <= window // dil)
    blk = jnp.arange(nb, dtype=jnp.int32)[:, None, None]
    valid = band[None] & ((blk > 0) | (kj[None] >= BLOCK))
    bias = rel_bias.astype(jnp.float32)[t5_bucket(jnp.maximum(dist, 0) * dil)]
    s = s + bias.transpose(2, 0, 1)[None, None, None]
    s = jnp.where(valid[None, None, :, None], s, NEG_INF)

    m = jnp.max(s, axis=-1, keepdims=True)
    e = jnp.exp(s - m)
    den = jnp.sum(e, axis=-1)
    o = jnp.einsum('brnhqk,brnkhd->brnqhd', e, vc) / den.transpose(0, 1, 2, 4, 3)[..., None]
    lse = (m[..., 0] + jnp.log(den)).transpose(0, 1, 2, 4, 3)

    o = o.reshape(B, dil, L, H, Dh).transpose(0, 2, 1, 3, 4).reshape(B, S_pad, H, Dh)[:, :S]
    lse = lse.reshape(B, dil, L, H).transpose(0, 2, 1, 3).reshape(B, S_pad, H)[:, :S]
    return o, lse


def dilated_attention(q, k, v, rel_bias):
    q, k, v = (t.astype(jnp.float32) for t in (q, k, v))
    outs, lses = [], []
    for window, dil in DILATED_PATTERNS:
        o, lse = dilated_branch(q, k, v, rel_bias, window, dil)
        outs.append(o)
        lses.append(lse)
    w = jax.nn.softmax(jnp.stack(lses, axis=0), axis=0)
    return jnp.sum(w[..., None] * jnp.stack(outs, axis=0), axis=0)


def swiglu(h, w_gate, w_up, w_down):
    return (jax.nn.silu(h @ w_gate) * (h @ w_up)) @ w_down


def setup_inputs(seed: int = 0) -> dict:
    key = jax.random.key(seed)
    ks = jax.random.split(key, 24)
    f32 = jnp.float32

    def nrm(k, shape, scale):
        return jax.random.normal(k, shape, f32) * scale

    a8 = jax.random.uniform(ks[13], (DEPTH, REC_WIDTH), f32, 0.9, 0.999)
    a = a8 ** (1.0 / LRU_C)
    lru_lambda = jnp.log(a) - jnp.log1p(-a)
    return {
        "x": nrm(ks[0], (BATCH, SEQ, D_MODEL), 1.0),
        "c": nrm(ks[1], (BATCH, D_MODEL), 1.0),
        "ada_w": nrm(ks[2], (DEPTH, D_MODEL, N_ADA * D_MODEL), 0.5 * D_MODEL ** -0.5),
        "ada_b": nrm(ks[3], (DEPTH, N_ADA * D_MODEL), 0.01),
        "norm1_g": 1.0 + nrm(ks[4], (DEPTH, D_MODEL), 0.02),
        "norm2_g": 1.0 + nrm(ks[5], (DEPTH, D_MODEL), 0.02),
        "w_in": nrm(ks[6], (DEPTH, D_MODEL, 2 * REC_WIDTH + 3 * ATTN_WIDTH), D_MODEL ** -0.5),
        "conv_w": nrm(ks[7], (DEPTH, CONV_WIDTH, REC_WIDTH), CONV_WIDTH ** -0.5),
        "conv_b": nrm(ks[8], (DEPTH, REC_WIDTH), 0.01),
        "rg_w_a": nrm(ks[9], (DEPTH, N_REC_HEADS, REC_HEAD_DIM, REC_HEAD_DIM), REC_HEAD_DIM ** -0.5),
        "rg_b_a": nrm(ks[10], (DEPTH, REC_WIDTH), 0.01),
        "rg_w_i": nrm(ks[11], (DEPTH, N_REC_HEADS, REC_HEAD_DIM, REC_HEAD_DIM), REC_HEAD_DIM ** -0.5),
        "rg_b_i": nrm(ks[12], (DEPTH, REC_WIDTH), 0.01),
        "lru_lambda": lru_lambda,
        "rel_bias": nrm(ks[14], (REL_BUCKETS, N_ATTN_HEADS), 0.5),
        "gnorm_rec": 1.0 + nrm(ks[15], (DEPTH, REC_WIDTH), 0.02),
        "gnorm_attn": 1.0 + nrm(ks[16], (DEPTH, ATTN_WIDTH), 0.02),
        "w_out": nrm(ks[17], (DEPTH, MIX_WIDTH, D_MODEL), MIX_WIDTH ** -0.5),
        "w_gate": nrm(ks[18], (DEPTH, D_MODEL, D_FF), D_MODEL ** -0.5),
        "w_up": nrm(ks[19], (DEPTH, D_MODEL, D_FF), D_MODEL ** -0.5),
        "w_down": nrm(ks[20], (DEPTH, D_FF, D_MODEL), D_FF ** -0.5),
        "final_g": 1.0 + nrm(ks[21], (D_MODEL,), 0.02),
    }


def reference(x, c, ada_w, ada_b, norm1_g, norm2_g, w_in, conv_w, conv_b, rg_w_a, rg_b_a,
              rg_w_i, rg_b_i, lru_lambda, rel_bias, gnorm_rec, gnorm_attn, w_out,
              w_gate, w_up, w_down, final_g):
    B, S, D = x.shape
    cond = jax.nn.silu(c)
    col_splits = [REC_WIDTH, 2 * REC_WIDTH, 2 * REC_WIDTH + ATTN_WIDTH, 2 * REC_WIDTH + 2 * ATTN_WIDTH]
    for l in range(DEPTH):
        mod = cond @ ada_w[l] + ada_b[l]
        sh1, sc1, g1, sh2, sc2, g2 = jnp.split(mod, N_ADA, axis=-1)

        h = rms_norm(x, norm1_g[l]) * (1.0 + sc1[:, None]) + sh1[:, None]
        proj = h @ w_in[l]
        xr, yg, q, k, v = jnp.split(proj, col_splits, axis=-1)
        xr = causal_conv(xr, conv_w[l], conv_b[l])
        rec = rg_lru(xr, rg_w_a[l], rg_b_a[l], rg_w_i[l], rg_b_i[l], lru_lambda[l])
        rec = rec * jax.nn.gelu(yg.astype(jnp.float32))
        att = dilated_attention(q.reshape(B, S, N_ATTN_HEADS, HEAD_DIM),
                                k.reshape(B, S, N_ATTN_HEADS, HEAD_DIM),
                                v.reshape(B, S, N_ATTN_HEADS, HEAD_DIM),
                                rel_bias).reshape(B, S, ATTN_WIDTH)
        mixed = jnp.concatenate([rms_norm(rec.astype(x.dtype), gnorm_rec[l]),
                                 rms_norm(att.astype(x.dtype), gnorm_attn[l])], axis=-1)
        x = x + g1[:, None] * (mixed @ w_out[l])

        h = rms_norm(x, norm2_g[l]) * (1.0 + sc2[:, None]) + sh2[:, None]
        x = x + g2[:, None] * swiglu(h, w_gate[l], w_up[l], w_down[l])
    return rms_norm(x, final_g)
```

```python
import functools
import math

import jax
import jax.numpy as jnp
from jax import lax
from jax.experimental import pallas as pl
from jax.experimental.pallas import tpu as pltpu

F32 = jnp.float32
BF16 = jnp.bfloat16

HEAD_DIM = 128
N_REC_HEADS = 16
CONV_WIDTH = 4
LRU_C = 8.0
DILATED_PATTERNS = ((128, 1), (512, 4), (2048, 16))
BLOCK = 128
REL_BUCKETS = 32
REL_MAX_DISTANCE = 2048
N_ADA = 6
EPS = 1e-6
NEG_INF = -1e30

SUBLANES = 8
VMEM_LIMIT = 56 * 1024 * 1024


def _params(*sem):
    return pltpu.CompilerParams(dimension_semantics=sem, vmem_limit_bytes=VMEM_LIMIT)


def _ada_kernel(c_ref, w_ref, b_ref, o_ref):
    c = c_ref[...]
    cond = (c * jax.nn.sigmoid(c)).astype(BF16)
    o_ref[...] = jnp.dot(cond, w_ref[...].astype(BF16), preferred_element_type=F32) + b_ref[...]


def _ada_mod(c, ada_w, ada_b, tn=512):
    B, D = c.shape
    N = ada_w.shape[1]
    rows = -(-B // SUBLANES) * SUBLANES
    c_pad = jnp.pad(c, ((0, rows - B), (0, 0)))
    out = pl.pallas_call(
        _ada_kernel,
        grid=(N // tn,),
        in_specs=[pl.BlockSpec((rows, D), lambda j: (0, 0)),
                  pl.BlockSpec((D, tn), lambda j: (0, j)),
                  pl.BlockSpec((1, tn), lambda j: (0, j))],
        out_specs=pl.BlockSpec((rows, tn), lambda j: (0, j)),
        out_shape=jax.ShapeDtypeStruct((rows, N), F32),
        compiler_params=_params("arbitrary"),
        name="ada_mod",
    )(c_pad, ada_w, ada_b.reshape(1, N))
    return out[:B]


def _norm_kernel(x_ref, g_ref, sc_ref, sh_ref, o_ref):
    x = x_ref[...]
    ms = jnp.mean(x * x, axis=-1, keepdims=True)
    y = x * lax.rsqrt(ms + EPS) * g_ref[...]
    o_ref[...] = (y * (1.0 + sc_ref[...]) + sh_ref[...]).astype(o_ref.dtype)


def _norm_mod(x, g, sc, sh, out_dtype, ts=256):
    B, S, D = x.shape
    return pl.pallas_call(
        _norm_kernel,
        grid=(B, S // ts),
        in_specs=[pl.BlockSpec((None, ts, D), lambda b, t: (b, t, 0)),
                  pl.BlockSpec((1, D), lambda b, t: (0, 0)),
                  pl.BlockSpec((None, 1, D), lambda b, t: (b, 0, 0)),
                  pl.BlockSpec((None, 1, D), lambda b, t: (b, 0, 0))],
        out_specs=pl.BlockSpec((None, ts, D), lambda b, t: (b, t, 0)),
        out_shape=jax.ShapeDtypeStruct((B, S, D), out_dtype),
        compiler_params=_params("arbitrary", "arbitrary"),
        name="norm_mod",
    )(x, g.reshape(1, D), sc.reshape(B, 1, D), sh.reshape(B, 1, D))


def _mm_kernel(a_ref, w_ref, o_ref):
    o_ref[...] = jnp.dot(a_ref[...], w_ref[...], preferred_element_type=F32).astype(o_ref.dtype)


def _matmul(a, w, out_dtype, tm=1024, tn=1024):
    M, K = a.shape
    N = w.shape[1]
    return pl.pallas_call(
        _mm_kernel,
        grid=(M // tm, N // tn),
        in_specs=[pl.BlockSpec((tm, K), lambda i, j: (i, 0)),
                  pl.BlockSpec((K, tn), lambda i, j: (0, j))],
        out_specs=pl.BlockSpec((tm, tn), lambda i, j: (i, j)),
        out_shape=jax.ShapeDtypeStruct((M, N), out_dtype),
        compiler_params=_params("arbitrary", "arbitrary"),
        name="matmul",
    )(a, w)


def _rglru_kernel(xr_ref, yg_ref, cw_ref, cb_ref, wai_ref, ba_ref, bi_ref, lam_ref, gn_ref,
                  o_ref, xbuf, a_s, u_s, h_s, hc, *, ts, pitch):
    nh = xbuf.shape[0]
    hd = xbuf.shape[2]
    t_blk = pl.program_id(1)

    @pl.when(t_blk == 0)
    def _():
        xbuf[:, 0:SUBLANES, :] = jnp.zeros((nh, SUBLANES, hd), F32)
        hc[...] = jnp.zeros(hc.shape, F32)

    for c in range(nh):
        lanes = slice(c * hd, (c + 1) * hd)
        xbuf[c, SUBLANES:SUBLANES + ts, :] = xr_ref[:, lanes].astype(F32)
        y = cb_ref[:, lanes] + cw_ref[CONV_WIDTH - 1:CONV_WIDTH, lanes] * xbuf[c, SUBLANES:SUBLANES + ts, :]
        for j in range(CONV_WIDTH - 1):
            off = SUBLANES - (CONV_WIDTH - 1) + j
            y = y + cw_ref[j:j + 1, lanes] * xbuf[c, off:off + ts, :]
        xbuf[c, 0:SUBLANES, :] = xbuf[c, ts:ts + SUBLANES, :]
        g = jnp.dot(y.astype(BF16), wai_ref[c], preferred_element_type=F32)
        r = jax.nn.sigmoid(g[:, :hd] + ba_ref[:, lanes])
        i = jax.nn.sigmoid(g[:, hd:] + bi_ref[:, lanes])
        z = -lam_ref[:, lanes]
        softplus = jnp.maximum(z, 0.0) + jnp.log1p(jnp.exp(-jnp.abs(z)))
        log_a = (-LRU_C) * r * softplus
        a = jnp.exp(log_a)
        a_s[c * pitch:c * pitch + ts, :] = a
        u_s[c * pitch:c * pitch + ts, :] = jnp.sqrt(-jnp.tanh(log_a) * (a * a + 1.0)) * (i * y)

    ngrp = nh // SUBLANES

    def step(t, hs):
        new = []
        for gi in range(ngrp):
            rows = pl.ds(gi * SUBLANES * pitch + t, SUBLANES, stride=pitch)
            h = a_s[rows, :] * hs[gi] + u_s[rows, :]
            h_s[rows, :] = h
            new.append(h)
        return tuple(new)

    hs = lax.fori_loop(0, ts, step, tuple(hc[gi] for gi in range(ngrp)), unroll=8)
    for gi in range(ngrp):
        hc[gi] = hs[gi]

    ssq = jnp.zeros((ts, hd), F32)
    for c in range(nh):
        lanes = slice(c * hd, (c + 1) * hd)
        rec = h_s[c * pitch:c * pitch + ts, :] * jax.nn.gelu(yg_ref[:, lanes].astype(F32))
        a_s[c * pitch:c * pitch + ts, :] = rec
        ssq = ssq + rec * rec
    inv = lax.rsqrt(jnp.sum(ssq, axis=-1, keepdims=True) / (nh * hd) + EPS)
    for c in range(nh):
        lanes = slice(c * hd, (c + 1) * hd)
        o_ref[:, lanes] = (a_s[c * pitch:c * pitch + ts, :] * inv * gn_ref[:, lanes]).astype(o_ref.dtype)


def _rglru(rec_in, conv_w, conv_b, w_a, b_a, w_i, b_i, lam, gn, ts=256):
    B, S, R2 = rec_in.shape
    R = R2 // 2
    nh, hd = w_a.shape[0], w_a.shape[1]
    pitch = ts + SUBLANES
    wai = jnp.concatenate([w_a, w_i], axis=-1).astype(BF16)
    row = lambda v: v.reshape(1, R)
    vec = pl.BlockSpec((1, R), lambda b, t: (0, 0))
    return pl.pallas_call(
        functools.partial(_rglru_kernel, ts=ts, pitch=pitch),
        grid=(B, S // ts),
        in_specs=[pl.BlockSpec((None, ts, R), lambda b, t: (b, t, 0)),
                  pl.BlockSpec((None, ts, R), lambda b, t: (b, t, 1)),
                  pl.BlockSpec((CONV_WIDTH, R), lambda b, t: (0, 0)),
                  vec,
                  pl.BlockSpec((nh, hd, 2 * hd), lambda b, t: (0, 0, 0)),
                  vec, vec, vec, vec],
        out_specs=pl.BlockSpec((None, ts, R), lambda b, t: (b, t, 0)),
        out_shape=jax.ShapeDtypeStruct((B, S, R), BF16),
        scratch_shapes=[pltpu.VMEM((nh, ts + SUBLANES, hd), F32),
                        pltpu.VMEM((nh * pitch, hd), F32),
                        pltpu.VMEM((nh * pitch, hd), F32),
                        pltpu.VMEM((nh * pitch, hd), F32),
                        pltpu.VMEM((nh // SUBLANES, SUBLANES, hd), F32)],
        compiler_params=_params("arbitrary", "arbitrary"),
        name="rglru",
    )(rec_in, rec_in, conv_w, row(conv_b), wai, row(b_a), row(b_i), row(lam), row(gn))


def _t5_bucket(n):
    max_exact = REL_BUCKETS // 2
    nf = jnp.maximum(n, 1).astype(F32)
    large = max_exact + (jnp.log(nf / max_exact) / math.log(REL_MAX_DISTANCE / max_exact)
                         * (REL_BUCKETS - max_exact)).astype(jnp.int32)
    large = jnp.minimum(large, REL_BUCKETS - 1)
    return jnp.where(n < max_exact, n, large)


def _attn_bias(rel_bias):
    qi = jnp.arange(BLOCK, dtype=jnp.int32)[:, None]
    kj = jnp.arange(2 * BLOCK, dtype=jnp.int32)[None, :]
    dist = qi + BLOCK - kj
    tables = []
    for window, dil in DILATED_PATTERNS:
        band = (dist >= 0) & (dist <= window // dil)
        bias = rel_bias.astype(F32)[_t5_bucket(jnp.maximum(dist, 0) * dil)]
        tables.append(jnp.where(band[..., None], bias, NEG_INF).transpose(2, 0, 1))
    return jnp.stack(tables, axis=0)


def _attn_kernel(q_ref, k_ref, v_ref, bias_ref, o_ref, qs, ks, vs, ob, lb, *, S):
    scale = HEAD_DIM ** -0.5
    ks[0:BLOCK, :] = jnp.zeros((BLOCK, HEAD_DIM), BF16)
    vs[0:BLOCK, :] = jnp.zeros((BLOCK, HEAD_DIM), BF16)
    left = lax.broadcasted_iota(jnp.int32, (1, 2 * BLOCK), 1) < BLOCK

    for p, (_, dil) in enumerate(DILATED_PATTERNS):
        L = S // dil
        nbr = L // BLOCK
        for r in range(dil):
            rows = slice(None) if dil == 1 else pl.ds(r, L, stride=dil)
            qs[r * L:(r + 1) * L, :] = (q_ref[rows, :] * scale).astype(BF16)
            ks[BLOCK + r * L:BLOCK + (r + 1) * L, :] = k_ref[rows, :].astype(BF16)
            vs[BLOCK + r * L:BLOCK + (r + 1) * L, :] = v_ref[rows, :].astype(BF16)

        def block(n, carry, p=p, dil=dil, nbr=nbr):
            row0 = pl.multiple_of(n * BLOCK, BLOCK)
            q = qs[pl.ds(row0, BLOCK), :]
            kw = ks[pl.ds(row0, 2 * BLOCK), :]
            vw = vs[pl.ds(row0, 2 * BLOCK), :]
            s = lax.dot_general(q, kw, (((1,), (1,)), ((), ())), preferred_element_type=F32)
            first = (n % nbr) == 0
            pen = jnp.where(first, NEG_INF, 0.0)
            s = s + bias_ref[p] + jnp.where(left, pen, 0.0)
            m = jnp.max(s, axis=-1, keepdims=True)
            e = jnp.exp(s - m)
            den = jnp.sum(e, axis=-1, keepdims=True)
            o = jnp.dot(e.astype(BF16), vw, preferred_element_type=F32) / den
            lse = jnp.broadcast_to(m + jnp.log(den), (BLOCK, HEAD_DIM))
            if dil == 1:
                rows = pl.ds(row0, BLOCK)
            else:
                tok0 = (n % nbr) * (BLOCK * dil) + n // nbr
                rows = pl.ds(tok0, BLOCK, stride=dil)
            ob[p, rows, :] = o
            lb[p, rows, :] = lse
            return carry

        lax.fori_loop(0, S // BLOCK, block, 0)

    npat = len(DILATED_PATTERNS)
    ch = 512
    for c0 in range(0, S, ch):
        rows = slice(c0, c0 + ch)
        ls = [lb[p, rows, :] for p in range(npat)]
        mx = functools.reduce(jnp.maximum, ls)
        ws = [jnp.exp(l - mx) for l in ls]
        num = functools.reduce(lambda a, b: a + b, [w * ob[p, rows, :] for p, w in enumerate(ws)])
        o_ref[rows, :] = (num / functools.reduce(lambda a, b: a + b, ws)).astype(o_ref.dtype)


def _dilated_attention(qkv, rel_bias, n_heads):
    B, S, A3 = qkv.shape
    A = A3 // 3
    bias = _attn_bias(rel_bias)
    npat = len(DILATED_PATTERNS)
    head = lambda off: pl.BlockSpec((None, S, HEAD_DIM), lambda b, h: (b, 0, off + h))
    return pl.pallas_call(
        functools.partial(_attn_kernel, S=S),
        grid=(B, n_heads),
        in_specs=[head(0), head(n_heads), head(2 * n_heads),
                  pl.BlockSpec((npat, None, BLOCK, 2 * BLOCK), lambda b, h: (0, h, 0, 0))],
        out_specs=pl.BlockSpec((None, S, HEAD_DIM), lambda b, h: (b, 0, h)),
        out_shape=jax.ShapeDtypeStruct((B, S, A), BF16),
        scratch_shapes=[pltpu.VMEM((S, HEAD_DIM), BF16),
                        pltpu.VMEM((S + BLOCK, HEAD_DIM), BF16),
                        pltpu.VMEM((S + BLOCK, HEAD_DIM), BF16),
                        pltpu.VMEM((npat, S, HEAD_DIM), F32),
                        pltpu.VMEM((npat, S, HEAD_DIM), F32)],
        compiler_params=_params("arbitrary", "arbitrary"),
        name="dilated_attn",
    )(qkv, qkv, qkv, bias)


def _outproj_kernel(rec_ref, att_ref, gn_ref, w1_ref, w2_ref, x_ref, g_ref, o_ref, att_n):
    @pl.when(pl.program_id(1) == 0)
    def _():
        a = att_ref[...].astype(F32)
        ms = jnp.mean(a * a, axis=-1, keepdims=True)
        att_n[...] = (a * lax.rsqrt(ms + EPS) * gn_ref[...]).astype(BF16)

    acc = jnp.dot(rec_ref[...], w1_ref[...], preferred_element_type=F32)
    acc = acc + jnp.dot(att_n[...], w2_ref[...], preferred_element_type=F32)
    o_ref[...] = x_ref[...] + g_ref[...] * acc


def _out_proj(rec_n, att, gn_att, w_out, x, gate, tm=1024, tn=512):
    M, R = rec_n.shape
    A = att.shape[1]
    D = w_out.shape[1]
    B = gate.shape[0]
    S = M // B
    return pl.pallas_call(
        _outproj_kernel,
        grid=(M // tm, D // tn),
        in_specs=[pl.BlockSpec((tm, R), lambda i, j: (i, 0)),
                  pl.BlockSpec((tm, A), lambda i, j: (i, 0)),
                  pl.BlockSpec((1, A), lambda i, j: (0, 0)),
                  pl.BlockSpec((R, tn), lambda i, j: (0, j)),
                  pl.BlockSpec((A, tn), lambda i, j: (R // A, j)),
                  pl.BlockSpec((tm, tn), lambda i, j: (i, j)),
                  pl.BlockSpec((None, 1, tn), lambda i, j: (i * tm // S, 0, j))],
        out_specs=pl.BlockSpec((tm, tn), lambda i, j: (i, j)),
        out_shape=jax.ShapeDtypeStruct((M, D), F32),
        scratch_shapes=[pltpu.VMEM((tm, A), BF16)],
        compiler_params=_params("arbitrary", "arbitrary"),
        name="out_proj",
    )(rec_n, att, gn_att.reshape(1, A), w_out, w_out, x, gate.reshape(B, 1, D))


def _gateup_kernel(h_ref, wg_ref, wu_ref, o_ref):
    h = h_ref[...]
    g = jnp.dot(h, wg_ref[...], preferred_element_type=F32)
    u = jnp.dot(h, wu_ref[...], preferred_element_type=F32)
    o_ref[...] = (g * jax.nn.sigmoid(g) * u).astype(o_ref.dtype)


def _gate_up(h, w_gate, w_up, tm=1024, tf=256):
    M, D = h.shape
    F = w_gate.shape[1]
    return pl.pallas_call(
        _gateup_kernel,
        grid=(M // tm, F // tf),
        in_specs=[pl.BlockSpec((tm, D), lambda i, j: (i, 0)),
                  pl.BlockSpec((D, tf), lambda i, j: (0, j)),
                  pl.BlockSpec((D, tf), lambda i, j: (0, j))],
        out_specs=pl.BlockSpec((tm, tf), lambda i, j: (i, j)),
        out_shape=jax.ShapeDtypeStruct((M, F), BF16),
        compiler_params=_params("arbitrary", "arbitrary"),
        name="gate_up",
    )(h, w_gate, w_up)


def _down_kernel(a_ref, w_ref, x_ref, g_ref, o_ref):
    acc = jnp.dot(a_ref[...], w_ref[...], preferred_element_type=F32)
    o_ref[...] = x_ref[...] + g_ref[...] * acc


def _down(act, w_down, x, gate, tm=512, tn=512):
    M, F = act.shape
    D = w_down.shape[1]
    B = gate.shape[0]
    S = M // B
    return pl.pallas_call(
        _down_kernel,
        grid=(M // tm, D // tn),
        in_specs=[pl.BlockSpec((tm, F), lambda i, j: (i, 0)),
                  pl.BlockSpec((F, tn), lambda i, j: (0, j)),
                  pl.BlockSpec((tm, tn), lambda i, j: (i, j)),
                  pl.BlockSpec((None, 1, tn), lambda i, j: (i * tm // S, 0, j))],
        out_specs=pl.BlockSpec((tm, tn), lambda i, j: (i, j)),
        out_shape=jax.ShapeDtypeStruct((M, D), F32),
        compiler_params=_params("arbitrary", "arbitrary"),
        name="down_proj",
    )(act, w_down, x, gate.reshape(B, 1, D))


def kernel(x, c, ada_w, ada_b, norm1_g, norm2_g, w_in, conv_w, conv_b, rg_w_a, rg_b_a, rg_w_i, rg_b_i,
           lru_lambda, rel_bias, gnorm_rec, gnorm_attn, w_out, w_gate, w_up, w_down, final_g):
    B, S, D = x.shape
    depth = ada_w.shape[0]
    R = conv_w.shape[-1]
    A = gnorm_attn.shape[-1]
    n_heads = A // HEAD_DIM
    M = B * S
    for l in range(depth):
        mod = _ada_mod(c, ada_w[l], ada_b[l])
        sh1, sc1, g1, sh2, sc2, g2 = jnp.split(mod, N_ADA, axis=-1)

        h = _norm_mod(x, norm1_g[l], sc1, sh1, BF16).reshape(M, D)
        w_in_b = w_in[l].astype(BF16)
        rec_in = _matmul(h, w_in_b[:, :2 * R], BF16).reshape(B, S, 2 * R)
        qkv = _matmul(h, w_in_b[:, 2 * R:], F32).reshape(B, S, 3 * A)
        rec_n = _rglru(rec_in, conv_w[l], conv_b[l], rg_w_a[l], rg_b_a[l], rg_w_i[l], rg_b_i[l],
                       lru_lambda[l], gnorm_rec[l])
        att = _dilated_attention(qkv, rel_bias, n_heads)
        x = _out_proj(rec_n.reshape(M, R), att.reshape(M, A), gnorm_attn[l], w_out[l].astype(BF16),
                      x.reshape(M, D), g1).reshape(B, S, D)

        h = _norm_mod(x, norm2_g[l], sc2, sh2, BF16).reshape(M, D)
        act = _gate_up(h, w_gate[l].astype(BF16), w_up[l].astype(BF16))
        x = _down(act, w_down[l].astype(BF16), x.reshape(M, D), g2).reshape(B, S, D)
    zeros = jnp.zeros((B, D), F32)
    return _norm_mod(x, final_g, zeros, zeros, x.dtype)
```

```python
import functools
import math

import jax
import jax.numpy as jnp
from jax import lax
from jax.experimental import pallas as pl
from jax.experimental.pallas import tpu as pltpu

F32 = jnp.float32
BF16 = jnp.bfloat16

HEAD_DIM = 128
N_REC_HEADS = 16
CONV_WIDTH = 4
LRU_C = 8.0
DILATED_PATTERNS = ((128, 1), (512, 4), (2048, 16))
BLOCK = 128
REL_BUCKETS = 32
REL_MAX_DISTANCE = 2048
N_ADA = 6
EPS = 1e-6
NEG_INF = -1e30

SUBLANES = 8
VMEM_LIMIT = 56 * 1024 * 1024


def _params(*sem):
    return pltpu.CompilerParams(dimension_semantics=sem, vmem_limit_bytes=VMEM_LIMIT)


def _ada_kernel(c_ref, w_ref, b_ref, o_ref):
    c = c_ref[...]
    cond = (c * jax.nn.sigmoid(c)).astype(BF16)
    o_ref[...] = jnp.dot(cond, w_ref[...].astype(BF16), preferred_element_type=F32) + b_ref[...]


def _ada_mod(c, ada_w, ada_b, tn=512):
    B, D = c.shape
    N = ada_w.shape[1]
    rows = -(-B // SUBLANES) * SUBLANES
    c_pad = jnp.pad(c, ((0, rows - B), (0, 0)))
    out = pl.pallas_call(
        _ada_kernel,
        grid=(N // tn,),
        in_specs=[pl.BlockSpec((rows, D), lambda j: (0, 0)),
                  pl.BlockSpec((D, tn), lambda j: (0, j)),
                  pl.BlockSpec((1, tn), lambda j: (0, j))],
        out_specs=pl.BlockSpec((rows, tn), lambda j: (0, j)),
        out_shape=jax.ShapeDtypeStruct((rows, N), F32),
        compiler_params=_params("arbitrary"),
        name="ada_mod",
    )(c_pad, ada_w, ada_b.reshape(1, N))
    return out[:B]


def _norm_kernel(x_ref, g_ref, sc_ref, sh_ref, o_ref):
    x = x_ref[...]
    ms = jnp.mean(x * x, axis=-1, keepdims=True)
    y = x * lax.rsqrt(ms + EPS) * g_ref[...]
    o_ref[...] = (y * (1.0 + sc_ref[...]) + sh_ref[...]).astype(o_ref.dtype)


def _norm_mod(x, g, sc, sh, out_dtype, ts=256):
    B, S, D = x.shape
    return pl.pallas_call(
        _norm_kernel,
        grid=(B, S // ts),
        in_specs=[pl.BlockSpec((None, ts, D), lambda b, t: (b, t, 0)),
                  pl.BlockSpec((1, D), lambda b, t: (0, 0)),
                  pl.BlockSpec((None, 1, D), lambda b, t: (b, 0, 0)),
                  pl.BlockSpec((None, 1, D), lambda b, t: (b, 0, 0))],
        out_specs=pl.BlockSpec((None, ts, D), lambda b, t: (b, t, 0)),
        out_shape=jax.ShapeDtypeStruct((B, S, D), out_dtype),
        compiler_params=_params("arbitrary", "arbitrary"),
        name="norm_mod",
    )(x, g.reshape(1, D), sc.reshape(B, 1, D), sh.reshape(B, 1, D))


def _mm_kernel(a_ref, w_ref, o_ref):
    o_ref[...] = jnp.dot(a_ref[...], w_ref[...], preferred_element_type=F32).astype(o_ref.dtype)


def _matmul(a, w, out_dtype, tm=1024, tn=1024):
    M, K = a.shape
    N = w.shape[1]
    return pl.pallas_call(
        _mm_kernel,
        grid=(M // tm, N // tn),
        in_specs=[pl.BlockSpec((tm, K), lambda i, j: (i, 0)),
                  pl.BlockSpec((K, tn), lambda i, j: (0, j))],
        out_specs=pl.BlockSpec((tm, tn), lambda i, j: (i, j)),
        out_shape=jax.ShapeDtypeStruct((M, N), out_dtype),
        compiler_params=_params("arbitrary", "arbitrary"),
        name="matmul",
    )(a, w)


def _rglru_kernel(xr_ref, yg_ref, cw_ref, cb_ref, wai_ref, ba_ref, bi_ref, lam_ref, gn_ref,
                  o_ref, xbuf, a_s, u_s, h_s, hc, *, ts, pitch):
    nh = xbuf.shape[0]
    hd = xbuf.shape[2]
    t_blk = pl.program_id(1)

    @pl.when(t_blk == 0)
    def _():
        xbuf[:, 0:SUBLANES, :] = jnp.zeros((nh, SUBLANES, hd), F32)
        hc[...] = jnp.zeros(hc.shape, F32)

    for c in range(nh):
        lanes = slice(c * hd, (c + 1) * hd)
        xbuf[c, SUBLANES:SUBLANES + ts, :] = xr_ref[:, lanes].astype(F32)
        y = cb_ref[:, lanes] + cw_ref[CONV_WIDTH - 1:CONV_WIDTH, lanes] * xbuf[c, SUBLANES:SUBLANES + ts, :]
        for j in range(CONV_WIDTH - 1):
            off = SUBLANES - (CONV_WIDTH - 1) + j
            y = y + cw_ref[j:j + 1, lanes] * xbuf[c, off:off + ts, :]
        xbuf[c, 0:SUBLANES, :] = xbuf[c, ts:ts + SUBLANES, :]
        g = jnp.dot(y.astype(BF16), wai_ref[c], preferred_element_type=F32)
        r = jax.nn.sigmoid(g[:, :hd] + ba_ref[:, lanes])
        i = jax.nn.sigmoid(g[:, hd:] + bi_ref[:, lanes])
        z = -lam_ref[:, lanes]
        softplus = jnp.maximum(z, 0.0) + jnp.log1p(jnp.exp(-jnp.abs(z)))
        log_a = (-LRU_C) * r * softplus
        a = jnp.exp(log_a)
        a_s[c * pitch:c * pitch + ts, :] = a
        u_s[c * pitch:c * pitch + ts, :] = jnp.sqrt(-jnp.tanh(log_a) * (a * a + 1.0)) * (i * y)

    ngrp = nh // SUBLANES

    def step(t, hs):
        new = []
        for gi in range(ngrp):
            rows = pl.ds(gi * SUBLANES * pitch + t, SUBLANES, stride=pitch)
            h = a_s[rows, :] * hs[gi] + u_s[rows, :]
            h_s[rows, :] = h
            new.append(h)
        return tuple(new)

    hs = lax.fori_loop(0, ts, step, tuple(hc[gi] for gi in range(ngrp)), unroll=8)
    for gi in range(ngrp):
        hc[gi] = hs[gi]

    ssq = jnp.zeros((ts, hd), F32)
    for c in range(nh):
        lanes = slice(c * hd, (c + 1) * hd)
        rec = h_s[c * pitch:c * pitch + ts, :] * jax.nn.gelu(yg_ref[:, lanes].astype(F32))
        a_s[c * pitch:c * pitch + ts, :] = rec
        ssq = ssq + rec * rec
    inv = lax.rsqrt(jnp.sum(ssq, axis=-1, keepdims=True) / (nh * hd) + EPS)
    for c in range(nh):
        lanes = slice(c * hd, (c + 1) * hd)
        o_ref[:, lanes] = (a_s[c * pitch:c * pitch + ts, :] * inv * gn_ref[:, lanes]).astype(o_ref.dtype)


def _rglru(rec_in, conv_w, conv_b, w_a, b_a, w_i, b_i, lam, gn, ts=256):
    B, S, R2 = rec_in.shape
    R = R2 // 2
    nh, hd = w_a.shape[0], w_a.shape[1]
    pitch = ts + SUBLANES
    wai = jnp.concatenate([w_a, w_i], axis=-1).astype(BF16)
    row = lambda v: v.reshape(1, R)
    vec = pl.BlockSpec((1, R), lambda b, t: (0, 0))
    return pl.pallas_call(
        functools.partial(_rglru_kernel, ts=ts, pitch=pitch),
        grid=(B, S // ts),
        in_specs=[pl.BlockSpec((None, ts, R), lambda b, t: (b, t, 0)),
                  pl.BlockSpec((None, ts, R), lambda b, t: (b, t, 1)),
                  pl.BlockSpec((CONV_WIDTH, R), lambda b, t: (0, 0)),
                  vec,
                  pl.BlockSpec((nh, hd, 2 * hd), lambda b, t: (0, 0, 0)),
                  vec, vec, vec, vec],
        out_specs=pl.BlockSpec((None, ts, R), lambda b, t: (b, t, 0)),
        out_shape=jax.ShapeDtypeStruct((B, S, R), BF16),
        scratch_shapes=[pltpu.VMEM((nh, ts + SUBLANES, hd), F32),
                        pltpu.VMEM((nh * pitch, hd), F32),
                        pltpu.VMEM((nh * pitch, hd), F32),
                        pltpu.VMEM((nh * pitch, hd), F32),
                        pltpu.VMEM((nh // SUBLANES, SUBLANES, hd), F32)],
        compiler_params=_params("arbitrary", "arbitrary"),
        name="rglru",
    )(rec_in, rec_in, conv_w, row(conv_b), wai, row(b_a), row(b_i), row(lam), row(gn))


def _t5_bucket(n):
    max_exact = REL_BUCKETS // 2
    nf = jnp.maximum(n, 1).astype(F32)
    large = max_exact + (jnp.log(nf / max_exact) / math.log(REL_MAX_DISTANCE / max_exact)
                         * (REL_BUCKETS - max_exact)).astype(jnp.int32)
    large = jnp.minimum(large, REL_BUCKETS - 1)
    return jnp.where(n < max_exact, n, large)


def _bias_kernel(rb_ref, bucket_ref, o_ref):
    h = pl.program_id(0)
    bucket = bucket_ref[...]
    bias = jnp.full(bucket.shape, NEG_INF, F32)
    for b in range(REL_BUCKETS):
        bias = jnp.where(bucket == b, rb_ref[h, b], bias)
    o_ref[...] = bias


def _attn_bias(rel_bias):
    n_heads = rel_bias.shape[1]
    qi = jnp.arange(BLOCK, dtype=jnp.int32)[:, None]
    kj = jnp.arange(2 * BLOCK, dtype=jnp.int32)[None, :]
    dist = qi + BLOCK - kj
    buckets = []
    for window, dil in DILATED_PATTERNS:
        band = (dist >= 0) & (dist <= window // dil)
        bucket = jnp.where(band, _t5_bucket(jnp.maximum(dist, 0) * dil), -1)
        buckets.append(jnp.stack([bucket, jnp.where(kj >= BLOCK, bucket, -1)]))
    buckets = jnp.stack(buckets)
    shape = buckets.shape
    return pl.pallas_call(
        _bias_kernel,
        grid=(n_heads,),
        in_specs=[pl.BlockSpec(memory_space=pltpu.SMEM),
                  pl.BlockSpec(shape, lambda h: (0, 0, 0, 0))],
        out_specs=pl.BlockSpec((None,) + shape, lambda h: (h, 0, 0, 0, 0)),
        out_shape=jax.ShapeDtypeStruct((n_heads,) + shape, F32),
        compiler_params=_params("arbitrary"),
        name="attn_bias",
    )(rel_bias.astype(F32).T, buckets)


ATTN_COPY_ROWS = 256
ATTN_MERGE_ROWS = 128


def _attn_kernel(q_ref, k_ref, v_ref, bias_ref, o_ref, x4, qs, ks, vs, ob, lb, *, S, unroll):
    scale = HEAD_DIM ** -0.5
    S4, S16 = S // 4, S // 16
    CH = ATTN_COPY_ROWS
    srcs = (q_ref, k_ref, v_ref)
    dsts = ((qs, 0), (ks, BLOCK), (vs, BLOCK))
    muls = (scale, None, None)
    ks[0:BLOCK, :] = jnp.zeros((BLOCK, HEAD_DIM), BF16)
    vs[0:BLOCK, :] = jnp.zeros((BLOCK, HEAD_DIM), BF16)

    def blocks(p):
        nbr = (S // DILATED_PATTERNS[p][1]) // BLOCK

        def body(it, carry):
            ns = [it * unroll + u for u in range(unroll)]
            row0s = [pl.multiple_of(n * BLOCK, BLOCK) for n in ns]
            ss = []
            for n, row0 in zip(ns, row0s):
                q = qs[pl.ds(row0, BLOCK), :]
                kw = ks[pl.ds(row0, 2 * BLOCK), :]
                ss.append(lax.dot_general(q, kw, (((1,), (1,)), ((), ())), preferred_element_type=F32))
            es, stats = [], []
            for n, s in zip(ns, ss):
                s = s + bias_ref[p, jnp.where(n % nbr == 0, 1, 0)]
                m = jnp.max(s, axis=-1, keepdims=True)
                e = jnp.exp(s - m)
                es.append(e.astype(BF16))
                stats.append((m, jnp.sum(e, axis=-1, keepdims=True)))
            for n, row0, e, (m, den) in zip(ns, row0s, es, stats):
                vw = vs[pl.ds(row0, 2 * BLOCK), :]
                o = jnp.dot(e, vw, preferred_element_type=F32) / den
                lse = jnp.broadcast_to(m + jnp.log(den), (BLOCK, HEAD_DIM))
                if p < 2:
                    rows = pl.ds(row0, BLOCK)
                else:
                    r = n // nbr
                    rows = pl.ds((r % 4) * S4 + 4 * (n % nbr) * BLOCK + r // 4, BLOCK, stride=4)
                ob[p, rows, :] = o
                lb[p, rows, :] = lse
            return carry

        lax.fori_loop(0, S // BLOCK // unroll, body, 0)

    def copy1(i, carry):
        r0 = pl.multiple_of(i * CH, CH)
        for src, (dst, off), mul in zip(srcs, dsts, muls):
            x = src[pl.ds(r0, CH), :]
            dst[pl.ds(off + r0, CH), :] = (x if mul is None else x * mul).astype(BF16)
        return carry

    lax.fori_loop(0, S // CH, copy1, 0)
    blocks(0)

    n4 = S4 // CH

    def copy4(i, carry):
        r, c = i // n4, i % n4
        r0 = pl.multiple_of(r * S4 + c * CH, CH)
        for t, (src, (dst, off), mul) in enumerate(zip(srcs, dsts, muls)):
            x = src[pl.ds(r + 4 * c * CH, CH, stride=4), :]
            x = x if mul is None else x * mul
            x4[t, pl.ds(r0, CH), :] = x
            dst[pl.ds(off + r0, CH), :] = x.astype(BF16)
        return carry

    lax.fori_loop(0, 4 * n4, copy4, 0)
    blocks(1)

    n16 = S16 // CH

    def copy16(i, carry):
        r, c = i // n16, i % n16
        r0 = pl.multiple_of(r * S16 + c * CH, CH)
        for t, (dst, off) in enumerate(dsts):
            x = x4[t, pl.ds((r % 4) * S4 + r // 4 + 4 * c * CH, CH, stride=4), :]
            dst[pl.ds(off + r0, CH), :] = x.astype(BF16)
        return carry

    lax.fori_loop(0, 16 * n16, copy16, 0)
    blocks(2)

    CM = ATTN_MERGE_ROWS
    nm = S4 // CM

    def merge(i, carry):
        r, c = i // nm, i % nm
        tok = pl.ds(r + 4 * c * CM, CM, stride=4)
        run = pl.ds(pl.multiple_of(r * S4 + c * CM, CM), CM)
        ls = [lb[0, tok, :], lb[1, run, :], lb[2, run, :]]
        os_ = [ob[0, tok, :], ob[1, run, :], ob[2, run, :]]
        mx = jnp.maximum(jnp.maximum(ls[0], ls[1]), ls[2])
        ws = [jnp.exp(l - mx) for l in ls]
        num = ws[0] * os_[0] + ws[1] * os_[1] + ws[2] * os_[2]
        o_ref[tok, :] = num / (ws[0] + ws[1] + ws[2])
        return carry

    lax.fori_loop(0, 4 * nm, merge, 0)


def _dilated_attention(qkv, rel_bias, n_heads, unroll=8):
    B, S, A3 = qkv.shape
    A = A3 // 3
    assert tuple(d for _, d in DILATED_PATTERNS) == (1, 4, 16)
    assert all(w // d == BLOCK for w, d in DILATED_PATTERNS)
    assert S % (16 * ATTN_COPY_ROWS) == 0 and S % (BLOCK * unroll) == 0
    bias = _attn_bias(rel_bias)
    npat = len(DILATED_PATTERNS)
    head = lambda off: pl.BlockSpec((None, S, HEAD_DIM), lambda b, h: (b, 0, off + h))
    return pl.pallas_call(
        functools.partial(_attn_kernel, S=S, unroll=unroll),
        grid=(B, n_heads),
        in_specs=[head(0), head(n_heads), head(2 * n_heads),
                  pl.BlockSpec((None, npat, 2, BLOCK, 2 * BLOCK), lambda b, h: (h, 0, 0, 0, 0))],
        out_specs=pl.BlockSpec((None, S, HEAD_DIM), lambda b, h: (b, 0, h)),
        out_shape=jax.ShapeDtypeStruct((B, S, A), F32),
        scratch_shapes=[pltpu.VMEM((3, S, HEAD_DIM), F32),
                        pltpu.VMEM((S, HEAD_DIM), BF16),
                        pltpu.VMEM((S + BLOCK, HEAD_DIM), BF16),
                        pltpu.VMEM((S + BLOCK, HEAD_DIM), BF16),
                        pltpu.VMEM((npat, S, HEAD_DIM), F32),
                        pltpu.VMEM((npat, S, HEAD_DIM), F32)],
        compiler_params=_params("arbitrary", "arbitrary"),
        name="dilated_attn",
    )(qkv, qkv, qkv, bias)


def _outproj_kernel(rec_ref, att_ref, gn_ref, w1_ref, w2_ref, x_ref, g_ref, o_ref, att_n):
    @pl.when(pl.program_id(1) == 0)
    def _():
        a = att_ref[...].astype(F32)
        ms = jnp.mean(a * a, axis=-1, keepdims=True)
        att_n[...] = (a * lax.rsqrt(ms + EPS) * gn_ref[...]).astype(BF16)

    acc = jnp.dot(rec_ref[...], w1_ref[...], preferred_element_type=F32)
    acc = acc + jnp.dot(att_n[...], w2_ref[...], preferred_element_type=F32)
    o_ref[...] = x_ref[...] + g_ref[...] * acc


def _out_proj(rec_n, att, gn_att, w_out, x, gate, tm=1024, tn=512):
    M, R = rec_n.shape
    A = att.shape[1]
    D = w_out.shape[1]
    B = gate.shape[0]
    S = M // B
    return pl.pallas_call(
        _outproj_kernel,
        grid=(M // tm, D // tn),
        in_specs=[pl.BlockSpec((tm, R), lambda i, j: (i, 0)),
                  pl.BlockSpec((tm, A), lambda i, j: (i, 0)),
                  pl.BlockSpec((1, A), lambda i, j: (0, 0)),
                  pl.BlockSpec((R, tn), lambda i, j: (0, j)),
                  pl.BlockSpec((A, tn), lambda i, j: (R // A, j)),
                  pl.BlockSpec((tm, tn), lambda i, j: (i, j)),
                  pl.BlockSpec((None, 1, tn), lambda i, j: (i * tm // S, 0, j))],
        out_specs=pl.BlockSpec((tm, tn), lambda i, j: (i, j)),
        out_shape=jax.ShapeDtypeStruct((M, D), F32),
        scratch_shapes=[pltpu.VMEM((tm, A), BF16)],
        compiler_params=_params("arbitrary", "arbitrary"),
        name="out_proj",
    )(rec_n, att, gn_att.reshape(1, A), w_out, w_out, x, gate.reshape(B, 1, D))


def _gateup_kernel(h_ref, wg_ref, wu_ref, o_ref):
    h = h_ref[...]
    g = jnp.dot(h, wg_ref[...], preferred_element_type=F32)
    u = jnp.dot(h, wu_ref[...], preferred_element_type=F32)
    o_ref[...] = (g * jax.nn.sigmoid(g) * u).astype(o_ref.dtype)


def _gate_up(h, w_gate, w_up, tm=1024, tf=256):
    M, D = h.shape
    F = w_gate.shape[1]
    return pl.pallas_call(
        _gateup_kernel,
        grid=(M // tm, F // tf),
        in_specs=[pl.BlockSpec((tm, D), lambda i, j: (i, 0)),
                  pl.BlockSpec((D, tf), lambda i, j: (0, j)),
                  pl.BlockSpec((D, tf), lambda i, j: (0, j))],
        out_specs=pl.BlockSpec((tm, tf), lambda i, j: (i, j)),
        out_shape=jax.ShapeDtypeStruct((M, F), BF16),
        compiler_params=_params("arbitrary", "arbitrary"),
        name="gate_up",
    )(h, w_gate, w_up)


def _down_kernel(a_ref, w_ref, x_ref, g_ref, o_ref):
    acc = jnp.dot(a_ref[...], w_ref[...], preferred_element_type=F32)
    o_ref[...] = x_ref[...] + g_ref[...] * acc


def _down(act, w_down, x, gate, tm=512, tn=512):
    M, F = act.shape
    D = w_down.shape[1]
    B = gate.shape[0]
    S = M // B
    return pl.pallas_call(
        _down_kernel,
        grid=(M // tm, D // tn),
        in_specs=[pl.BlockSpec((tm, F), lambda i, j: (i, 0)),
                  pl.BlockSpec((F, tn), lambda i, j: (0, j)),
                  pl.BlockSpec((tm, tn), lambda i, j: (i, j)),
                  pl.BlockSpec((None, 1, tn), lambda i, j: (i * tm // S, 0, j))],
        out_specs=pl.BlockSpec((tm, tn), lambda i, j: (i, j)),
        out_shape=jax.ShapeDtypeStruct((M, D), F32),
        compiler_params=_params("arbitrary", "arbitrary"),
        name="down_proj",
    )(act, w_down, x, gate.reshape(B, 1, D))


def kernel(x, c, ada_w, ada_b, norm1_g, norm2_g, w_in, conv_w, conv_b, rg_w_a, rg_b_a, rg_w_i, rg_b_i,
           lru_lambda, rel_bias, gnorm_rec, gnorm_attn, w_out, w_gate, w_up, w_down, final_g):
    B, S, D = x.shape
    depth = ada_w.shape[0]
    R = conv_w.shape[-1]
    A = gnorm_attn.shape[-1]
    n_heads = A // HEAD_DIM
    M = B * S
    for l in range(depth):
        mod = _ada_mod(c, ada_w[l], ada_b[l])
        sh1, sc1, g1, sh2, sc2, g2 = jnp.split(mod, N_ADA, axis=-1)

        h = _norm_mod(x, norm1_g[l], sc1, sh1, BF16).reshape(M, D)
        w_in_b = w_in[l].astype(BF16)
        rec_in = _matmul(h, w_in_b[:, :2 * R], BF16).reshape(B, S, 2 * R)
        qkv = _matmul(h, w_in_b[:, 2 * R:], F32).reshape(B, S, 3 * A)
        rec_n = _rglru(rec_in, conv_w[l], conv_b[l], rg_w_a[l], rg_b_a[l], rg_w_i[l], rg_b_i[l],
                       lru_lambda[l], gnorm_rec[l])
        att = _dilated_attention(qkv, rel_bias, n_heads)
        x = _out_proj(rec_n.reshape(M, R), att.reshape(M, A), gnorm_attn[l], w_out[l].astype(BF16),
                      x.reshape(M, D), g1).reshape(B, S, D)

        h = _norm_mod(x, norm2_g[l], sc2, sh2, BF16).reshape(M, D)
        act = _gate_up(h, w_gate[l].astype(BF16), w_up[l].astype(BF16))
        x = _down(act, w_down[l].astype(BF16), x.reshape(M, D), g2).reshape(B, S, D)
    zeros = jnp.zeros((B, D), F32)
    return _norm_mod(x, final_g, zeros, zeros, x.dtype)
```

```python
import functools
import math

import jax
import jax.numpy as jnp
from jax import lax
from jax.experimental import pallas as pl
from jax.experimental.pallas import tpu as pltpu

F32 = jnp.float32
BF16 = jnp.bfloat16

HEAD_DIM = 128
N_REC_HEADS = 16
CONV_WIDTH = 4
LRU_C = 8.0
DILATED_PATTERNS = ((128, 1), (512, 4), (2048, 16))
BLOCK = 128
REL_BUCKETS = 32
REL_MAX_DISTANCE = 2048
N_ADA = 6
EPS = 1e-6
NEG_INF = -1e30

SUBLANES = 8
VMEM_LIMIT = 56 * 1024 * 1024


def _params(*sem):
    return pltpu.CompilerParams(dimension_semantics=sem, vmem_limit_bytes=VMEM_LIMIT)


def _ada_kernel(c_ref, w_ref, b_ref, o_ref):
    c = c_ref[...]
    cond = (c * jax.nn.sigmoid(c)).astype(BF16)
    o_ref[...] = jnp.dot(cond, w_ref[...].astype(BF16), preferred_element_type=F32) + b_ref[...]


def _ada_mod(c, ada_w, ada_b, tn=512):
    B, D = c.shape
    N = ada_w.shape[1]
    rows = -(-B // SUBLANES) * SUBLANES
    c_pad = jnp.pad(c, ((0, rows - B), (0, 0)))
    out = pl.pallas_call(
        _ada_kernel,
        grid=(N // tn,),
        in_specs=[pl.BlockSpec((rows, D), lambda j: (0, 0)),
                  pl.BlockSpec((D, tn), lambda j: (0, j)),
                  pl.BlockSpec((1, tn), lambda j: (0, j))],
        out_specs=pl.BlockSpec((rows, tn), lambda j: (0, j)),
        out_shape=jax.ShapeDtypeStruct((rows, N), F32),
        compiler_params=_params("arbitrary"),
        name="ada_mod",
    )(c_pad, ada_w, ada_b.reshape(1, N))
    return out[:B]


def _norm_kernel(x_ref, g_ref, sc_ref, sh_ref, o_ref):
    x = x_ref[...]
    ms = jnp.mean(x * x, axis=-1, keepdims=True)
    y = x * lax.rsqrt(ms + EPS) * g_ref[...]
    o_ref[...] = (y * (1.0 + sc_ref[...]) + sh_ref[...]).astype(o_ref.dtype)


def _norm_mod(x, g, sc, sh, out_dtype, ts=256):
    B, S, D = x.shape
    return pl.pallas_call(
        _norm_kernel,
        grid=(B, S // ts),
        in_specs=[pl.BlockSpec((None, ts, D), lambda b, t: (b, t, 0)),
                  pl.BlockSpec((1, D), lambda b, t: (0, 0)),
                  pl.BlockSpec((None, 1, D), lambda b, t: (b, 0, 0)),
                  pl.BlockSpec((None, 1, D), lambda b, t: (b, 0, 0))],
        out_specs=pl.BlockSpec((None, ts, D), lambda b, t: (b, t, 0)),
        out_shape=jax.ShapeDtypeStruct((B, S, D), out_dtype),
        compiler_params=_params("arbitrary", "arbitrary"),
        name="norm_mod",
    )(x, g.reshape(1, D), sc.reshape(B, 1, D), sh.reshape(B, 1, D))


def _inproj_kernel(a_ref, w_ref, rec_ref, qkv_ref):
    acc = jnp.dot(a_ref[...], w_ref[...].astype(BF16), preferred_element_type=F32)
    rec_ref[...] = acc.astype(rec_ref.dtype)
    qkv_ref[...] = acc


INPROJ_TN = 512


def _in_proj(h, w_in, n_rec_cols, tm=1024, tn=INPROJ_TN):
    M, K = h.shape
    N = w_in.shape[1]
    n_rec = n_rec_cols // tn
    return pl.pallas_call(
        _inproj_kernel,
        grid=(M // tm, N // tn),
        in_specs=[pl.BlockSpec((tm, K), lambda i, j: (i, 0)),
                  pl.BlockSpec((K, tn), lambda i, j: (0, j))],
        out_specs=[pl.BlockSpec((tm, tn), lambda i, j: (i, jnp.minimum(j, n_rec))),
                   pl.BlockSpec((tm, tn), lambda i, j: (i, jnp.maximum(j - n_rec + 1, 0)))],
        out_shape=[jax.ShapeDtypeStruct((M, n_rec_cols + tn), BF16),
                   jax.ShapeDtypeStruct((M, tn + N - n_rec_cols), F32)],
        compiler_params=_params("arbitrary", "arbitrary"),
        name="in_proj",
    )(h, w_in)


def _rglru_kernel(xr_ref, yg_ref, cw_ref, cb_ref, wai_ref, ba_ref, bi_ref, lam_ref, gn_ref,
                  o_ref, xbuf, a_s, u_s, h_s, hc, *, ts, pitch):
    nh = xbuf.shape[0]
    hd = xbuf.shape[2]
    t_blk = pl.program_id(1)

    @pl.when(t_blk == 0)
    def _():
        xbuf[:, 0:SUBLANES, :] = jnp.zeros((nh, SUBLANES, hd), F32)
        hc[...] = jnp.zeros(hc.shape, F32)

    for c in range(nh):
        lanes = slice(c * hd, (c + 1) * hd)
        xbuf[c, SUBLANES:SUBLANES + ts, :] = xr_ref[:, lanes].astype(F32)
        y = cb_ref[:, lanes] + cw_ref[CONV_WIDTH - 1:CONV_WIDTH, lanes] * xbuf[c, SUBLANES:SUBLANES + ts, :]
        for j in range(CONV_WIDTH - 1):
            off = SUBLANES - (CONV_WIDTH - 1) + j
            y = y + cw_ref[j:j + 1, lanes] * xbuf[c, off:off + ts, :]
        xbuf[c, 0:SUBLANES, :] = xbuf[c, ts:ts + SUBLANES, :]
        g = jnp.dot(y.astype(BF16), wai_ref[c], preferred_element_type=F32)
        r = jax.nn.sigmoid(g[:, :hd] + ba_ref[:, lanes])
        i = jax.nn.sigmoid(g[:, hd:] + bi_ref[:, lanes])
        z = -lam_ref[:, lanes]
        softplus = jnp.maximum(z, 0.0) + jnp.log1p(jnp.exp(-jnp.abs(z)))
        log_a = r * ((-LRU_C) * softplus)
        a = jnp.exp(log_a)
        a_s[c * pitch:c * pitch + ts, :] = a
        v = -jnp.tanh(log_a) * (a * a + 1.0)
        u_s[c * pitch:c * pitch + ts, :] = jnp.where(v > 0.0, v * lax.rsqrt(v), 0.0) * (i * y)

    ngrp = nh // SUBLANES

    def step(t, hs):
        new = []
        for gi in range(ngrp):
            rows = pl.ds(gi * SUBLANES * pitch + t, SUBLANES, stride=pitch)
            h = a_s[rows, :] * hs[gi] + u_s[rows, :]
            h_s[rows, :] = h
            new.append(h)
        return tuple(new)

    hs = lax.fori_loop(0, ts, step, tuple(hc[gi] for gi in range(ngrp)), unroll=8)
    for gi in range(ngrp):
        hc[gi] = hs[gi]

    ssq = jnp.zeros((ts, hd), F32)
    for c in range(nh):
        lanes = slice(c * hd, (c + 1) * hd)
        rec = h_s[c * pitch:c * pitch + ts, :] * jax.nn.gelu(yg_ref[:, lanes].astype(F32))
        a_s[c * pitch:c * pitch + ts, :] = rec
        ssq = ssq + rec * rec
    inv = lax.rsqrt(jnp.sum(ssq, axis=-1, keepdims=True) / (nh * hd) + EPS)
    for c in range(nh):
        lanes = slice(c * hd, (c + 1) * hd)
        o_ref[:, lanes] = (a_s[c * pitch:c * pitch + ts, :] * inv * gn_ref[:, lanes]).astype(o_ref.dtype)


def _rglru(rec_in, conv_w, conv_b, w_a, b_a, w_i, b_i, lam, gn, ts=256):
    B, S, _ = rec_in.shape
    R = conv_w.shape[-1]
    nh, hd = w_a.shape[0], w_a.shape[1]
    pitch = ts + SUBLANES
    wai = jnp.concatenate([w_a, w_i], axis=-1).astype(BF16)
    row = lambda v: v.reshape(1, R)
    vec = pl.BlockSpec((1, R), lambda b, t: (0, 0))
    return pl.pallas_call(
        functools.partial(_rglru_kernel, ts=ts, pitch=pitch),
        grid=(B, S // ts),
        in_specs=[pl.BlockSpec((None, ts, R), lambda b, t: (b, t, 0)),
                  pl.BlockSpec((None, ts, R), lambda b, t: (b, t, 1)),
                  pl.BlockSpec((CONV_WIDTH, R), lambda b, t: (0, 0)),
                  vec,
                  pl.BlockSpec((nh, hd, 2 * hd), lambda b, t: (0, 0, 0)),
                  vec, vec, vec, vec],
        out_specs=pl.BlockSpec((None, ts, R), lambda b, t: (b, t, 0)),
        out_shape=jax.ShapeDtypeStruct((B, S, R), BF16),
        scratch_shapes=[pltpu.VMEM((nh, ts + SUBLANES, hd), F32),
                        pltpu.VMEM((nh * pitch, hd), F32),
                        pltpu.VMEM((nh * pitch, hd), F32),
                        pltpu.VMEM((nh * pitch, hd), F32),
                        pltpu.VMEM((nh // SUBLANES, SUBLANES, hd), F32)],
        compiler_params=_params("arbitrary", "arbitrary"),
        name="rglru",
    )(rec_in, rec_in, conv_w, row(conv_b), wai, row(b_a), row(b_i), row(lam), row(gn))


def _t5_bucket(n):
    max_exact = REL_BUCKETS // 2
    nf = jnp.maximum(n, 1).astype(F32)
    large = max_exact + (jnp.log(nf / max_exact) / math.log(REL_MAX_DISTANCE / max_exact)
                         * (REL_BUCKETS - max_exact)).astype(jnp.int32)
    large = jnp.minimum(large, REL_BUCKETS - 1)
    return jnp.where(n < max_exact, n, large)


def _bias_kernel(rb_ref, bucket_ref, o_ref):
    h = pl.program_id(0)
    bucket = bucket_ref[...]
    bias = jnp.full(bucket.shape, NEG_INF, F32)
    for b in range(REL_BUCKETS):
        bias = jnp.where(bucket == b, rb_ref[h, b], bias)
    o_ref[...] = bias


def _attn_bias(rel_bias):
    n_heads = rel_bias.shape[1]
    qi = jnp.arange(BLOCK, dtype=jnp.int32)[:, None]
    kj = jnp.arange(2 * BLOCK, dtype=jnp.int32)[None, :]
    dist = qi + BLOCK - kj
    buckets = []
    for window, dil in DILATED_PATTERNS:
        band = (dist >= 0) & (dist <= window // dil)
        bucket = jnp.where(band, _t5_bucket(jnp.maximum(dist, 0) * dil), -1)
        buckets.append(jnp.stack([bucket, jnp.where(kj >= BLOCK, bucket, -1)]))
    buckets = jnp.stack(buckets)
    shape = buckets.shape
    return pl.pallas_call(
        _bias_kernel,
        grid=(n_heads,),
        in_specs=[pl.BlockSpec(memory_space=pltpu.SMEM),
                  pl.BlockSpec(shape, lambda h: (0, 0, 0, 0))],
        out_specs=pl.BlockSpec((None,) + shape, lambda h: (h, 0, 0, 0, 0)),
        out_shape=jax.ShapeDtypeStruct((n_heads,) + shape, F32),
        compiler_params=_params("arbitrary"),
        name="attn_bias",
    )(rel_bias.astype(F32).T, buckets)


ATTN_COPY_ROWS = 256
ATTN_MERGE_ROWS = 128


def _attn_kernel(q_ref, k_ref, v_ref, bias_ref, o_ref, x4, qs0, ks0, vs0, qs1, ks1, vs1, ob, lb,
                 *, S, unroll):
    scale = HEAD_DIM ** -0.5
    S4, S16 = S // 4, S // 16
    CH = ATTN_COPY_ROWS
    n_it = S // BLOCK // unroll
    srcs = (q_ref, k_ref, v_ref)
    offs = (0, BLOCK, BLOCK)
    muls = (scale, None, None)
    qs, ks, vs = (qs0, qs1), (ks0, ks1), (vs0, vs1)
    for st in range(2):
        ks[st][0:BLOCK, :] = jnp.zeros((BLOCK, HEAD_DIM), BF16)
        vs[st][0:BLOCK, :] = jnp.zeros((BLOCK, HEAD_DIM), BF16)

    def blocks(p, extra=None):
        nbr = (S // DILATED_PATTERNS[p][1]) // BLOCK
        st = p % 2

        def body(it, carry):
            ns = [it * unroll + u for u in range(unroll)]
            row0s = [pl.multiple_of(n * BLOCK, BLOCK) for n in ns]
            ss = []
            for n, row0 in zip(ns, row0s):
                q = qs[st][pl.ds(row0, BLOCK), :]
                kw = ks[st][pl.ds(row0, 2 * BLOCK), :]
                ss.append(lax.dot_general(q, kw, (((1,), (1,)), ((), ())), preferred_element_type=F32))
            if extra is not None:
                extra(it)
            es, stats = [], []
            for n, s in zip(ns, ss):
                s = s + bias_ref[p, jnp.where(n % nbr == 0, 1, 0)]
                m = jnp.max(s, axis=-1, keepdims=True)
                e = jnp.exp(s - m)
                es.append(e.astype(BF16))
                stats.append((m, jnp.sum(e, axis=-1, keepdims=True)))
            for n, row0, e, (m, den) in zip(ns, row0s, es, stats):
                vw = vs[st][pl.ds(row0, 2 * BLOCK), :]
                o = jnp.dot(e, vw, preferred_element_type=F32) / den
                lse = jnp.broadcast_to(m + jnp.log(den), (BLOCK, HEAD_DIM))
                if p < 2:
                    rows = pl.ds(row0, BLOCK)
                else:
                    r = n // nbr
                    rows = pl.ds((r % 4) * S4 + 4 * (n % nbr) * BLOCK + r // 4, BLOCK, stride=4)
                ob[p, rows, :] = o
                lb[p, rows, :] = lse
            return carry

        lax.fori_loop(0, n_it, body, 0)

    dsts = (qs, ks, vs)

    def copy1(i, carry):
        r0 = pl.multiple_of(i * CH, CH)
        for src, dst, off, mul in zip(srcs, dsts, offs, muls):
            x = src[pl.ds(r0, CH), :]
            dst[0][pl.ds(off + r0, CH), :] = (x if mul is None else x * mul).astype(BF16)
        return carry

    lax.fori_loop(0, S // CH, copy1, 0)

    n4 = S4 // CH

    def copy4(i):
        r, c = i // n4, i % n4
        r0 = pl.multiple_of(r * S4 + c * CH, CH)
        for t, (src, dst, off, mul) in enumerate(zip(srcs, dsts, offs, muls)):
            x = src[pl.ds(r + 4 * c * CH, CH, stride=4), :]
            x = x if mul is None else x * mul
            x4[t, pl.ds(r0, CH), :] = x
            dst[1][pl.ds(off + r0, CH), :] = x.astype(BF16)

    n16 = S16 // CH

    def copy16(i):
        r, c = i // n16, i % n16
        r0 = pl.multiple_of(r * S16 + c * CH, CH)
        for t, (dst, off) in enumerate(zip(dsts, offs)):
            x = x4[t, pl.ds((r % 4) * S4 + r // 4 + 4 * c * CH, CH, stride=4), :]
            dst[0][pl.ds(off + r0, CH), :] = x.astype(BF16)

    def spread(copy, total):
        per_it = total // n_it
        return lambda it: [copy(it * per_it + s) for s in range(per_it)]

    blocks(0, spread(copy4, 4 * n4))
    blocks(1, spread(copy16, 16 * n16))
    blocks(2)

    CM = ATTN_MERGE_ROWS
    nm = S4 // CM

    def merge(i, carry):
        r, c = i // nm, i % nm
        tok = pl.ds(r + 4 * c * CM, CM, stride=4)
        run = pl.ds(pl.multiple_of(r * S4 + c * CM, CM), CM)
        ls = [lb[0, tok, :], lb[1, run, :], lb[2, run, :]]
        os_ = [ob[0, tok, :], ob[1, run, :], ob[2, run, :]]
        mx = jnp.maximum(jnp.maximum(ls[0], ls[1]), ls[2])
        ws = [jnp.exp(l - mx) for l in ls]
        num = ws[0] * os_[0] + ws[1] * os_[1] + ws[2] * os_[2]
        o_ref[tok, :] = num / (ws[0] + ws[1] + ws[2])
        return carry

    lax.fori_loop(0, 4 * nm, merge, 0)


def _dilated_attention(qkv, rel_bias, n_heads, col0=0, unroll=16):
    B, S, _ = qkv.shape
    A = n_heads * HEAD_DIM
    blk0 = col0 // HEAD_DIM
    assert tuple(d for _, d in DILATED_PATTERNS) == (1, 4, 16)
    assert all(w // d == BLOCK for w, d in DILATED_PATTERNS)
    assert S % (16 * ATTN_COPY_ROWS) == 0 and S % (BLOCK * unroll) == 0
    bias = _attn_bias(rel_bias)
    npat = len(DILATED_PATTERNS)
    head = lambda off: pl.BlockSpec((None, S, HEAD_DIM), lambda b, h: (b, 0, blk0 + off + h))
    return pl.pallas_call(
        functools.partial(_attn_kernel, S=S, unroll=unroll),
        grid=(B, n_heads),
        in_specs=[head(0), head(n_heads), head(2 * n_heads),
                  pl.BlockSpec((None, npat, 2, BLOCK, 2 * BLOCK), lambda b, h: (h, 0, 0, 0, 0))],
        out_specs=pl.BlockSpec((None, S, HEAD_DIM), lambda b, h: (b, 0, h)),
        out_shape=jax.ShapeDtypeStruct((B, S, A), F32),
        scratch_shapes=[pltpu.VMEM((3, S, HEAD_DIM), F32),
                        pltpu.VMEM((S, HEAD_DIM), BF16),
                        pltpu.VMEM((S + BLOCK, HEAD_DIM), BF16),
                        pltpu.VMEM((S + BLOCK, HEAD_DIM), BF16),
                        pltpu.VMEM((S, HEAD_DIM), BF16),
                        pltpu.VMEM((S + BLOCK, HEAD_DIM), BF16),
                        pltpu.VMEM((S + BLOCK, HEAD_DIM), BF16),
                        pltpu.VMEM((npat, S, HEAD_DIM), F32),
                        pltpu.VMEM((npat, S, HEAD_DIM), F32)],
        compiler_params=_params("arbitrary", "arbitrary"),
        name="dilated_attn",
    )(qkv, qkv, qkv, bias)


def _outproj_kernel(rec_ref, att_ref, gn_ref, w1_ref, w2_ref, x_ref, g_ref, o_ref, att_n):
    @pl.when(pl.program_id(1) == 0)
    def _():
        a = att_ref[...].astype(F32)
        ms = jnp.mean(a * a, axis=-1, keepdims=True)
        att_n[...] = (a * lax.rsqrt(ms + EPS) * gn_ref[...]).astype(BF16)

    acc = jnp.dot(rec_ref[...], w1_ref[...], preferred_element_type=F32)
    acc = acc + jnp.dot(att_n[...], w2_ref[...], preferred_element_type=F32)
    o_ref[...] = x_ref[...] + g_ref[...] * acc


def _out_proj(rec_n, att, gn_att, w_out, x, gate, tm=1024, tn=512):
    M, R = rec_n.shape
    A = att.shape[1]
    D = w_out.shape[1]
    B = gate.shape[0]
    S = M // B
    return pl.pallas_call(
        _outproj_kernel,
        grid=(M // tm, D // tn),
        in_specs=[pl.BlockSpec((tm, R), lambda i, j: (i, 0)),
                  pl.BlockSpec((tm, A), lambda i, j: (i, 0)),
                  pl.BlockSpec((1, A), lambda i, j: (0, 0)),
                  pl.BlockSpec((R, tn), lambda i, j: (0, j)),
                  pl.BlockSpec((A, tn), lambda i, j: (R // A, j)),
                  pl.BlockSpec((tm, tn), lambda i, j: (i, j)),
                  pl.BlockSpec((None, 1, tn), lambda i, j: (i * tm // S, 0, j))],
        out_specs=pl.BlockSpec((tm, tn), lambda i, j: (i, j)),
        out_shape=jax.ShapeDtypeStruct((M, D), F32),
        scratch_shapes=[pltpu.VMEM((tm, A), BF16)],
        compiler_params=_params("arbitrary", "arbitrary"),
        name="out_proj",
    )(rec_n, att, gn_att.reshape(1, A), w_out, w_out, x, gate.reshape(B, 1, D))


def _gateup_kernel(h_ref, wg_ref, wu_ref, o_ref):
    h = h_ref[...]
    g = jnp.dot(h, wg_ref[...].astype(BF16), preferred_element_type=F32)
    u = jnp.dot(h, wu_ref[...].astype(BF16), preferred_element_type=F32)
    o_ref[...] = (g * jax.nn.sigmoid(g) * u).astype(o_ref.dtype)


def _gate_up(h, w_gate, w_up, tm=1024, tf=256):
    M, D = h.shape
    F = w_gate.shape[1]
    return pl.pallas_call(
        _gateup_kernel,
        grid=(M // tm, F // tf),
        in_specs=[pl.BlockSpec((tm, D), lambda i, j: (i, 0)),
                  pl.BlockSpec((D, tf), lambda i, j: (0, j)),
                  pl.BlockSpec((D, tf), lambda i, j: (0, j))],
        out_specs=pl.BlockSpec((tm, tf), lambda i, j: (i, j)),
        out_shape=jax.ShapeDtypeStruct((M, F), BF16),
        compiler_params=_params("arbitrary", "arbitrary"),
        name="gate_up",
    )(h, w_gate, w_up)


def _down_kernel(a_ref, w_ref, x_ref, g_ref, o_ref):
    acc = jnp.dot(a_ref[...], w_ref[...], preferred_element_type=F32)
    o_ref[...] = x_ref[...] + g_ref[...] * acc


def _down(act, w_down, x, gate, tm=512, tn=512):
    M, F = act.shape
    D = w_down.shape[1]
    B = gate.shape[0]
    S = M // B
    return pl.pallas_call(
        _down_kernel,
        grid=(M // tm, D // tn),
        in_specs=[pl.BlockSpec((tm, F), lambda i, j: (i, 0)),
                  pl.BlockSpec((F, tn), lambda i, j: (0, j)),
                  pl.BlockSpec((tm, tn), lambda i, j: (i, j)),
                  pl.BlockSpec((None, 1, tn), lambda i, j: (i * tm // S, 0, j))],
        out_specs=pl.BlockSpec((tm, tn), lambda i, j: (i, j)),
        out_shape=jax.ShapeDtypeStruct((M, D), F32),
        compiler_params=_params("arbitrary", "arbitrary"),
        name="down_proj",
    )(act, w_down, x, gate.reshape(B, 1, D))


def kernel(x, c, ada_w, ada_b, norm1_g, norm2_g, w_in, conv_w, conv_b, rg_w_a, rg_b_a, rg_w_i, rg_b_i,
           lru_lambda, rel_bias, gnorm_rec, gnorm_attn, w_out, w_gate, w_up, w_down, final_g):
    B, S, D = x.shape
    depth = ada_w.shape[0]
    R = conv_w.shape[-1]
    A = gnorm_attn.shape[-1]
    n_heads = A // HEAD_DIM
    M = B * S
    for l in range(depth):
        mod = _ada_mod(c, ada_w[l], ada_b[l])
        sh1, sc1, g1, sh2, sc2, g2 = jnp.split(mod, N_ADA, axis=-1)

        h = _norm_mod(x, norm1_g[l], sc1, sh1, BF16).reshape(M, D)
        rec_in, qkv = _in_proj(h, w_in[l], 2 * R)
        rec_in = rec_in.reshape(B, S, rec_in.shape[-1])
        qkv = qkv.reshape(B, S, qkv.shape[-1])
        rec_n = _rglru(rec_in, conv_w[l], conv_b[l], rg_w_a[l], rg_b_a[l], rg_w_i[l], rg_b_i[l],
                       lru_lambda[l], gnorm_rec[l])
        att = _dilated_attention(qkv, rel_bias, n_heads, col0=INPROJ_TN)
        x = _out_proj(rec_n.reshape(M, R), att.reshape(M, A), gnorm_attn[l], w_out[l].astype(BF16),
                      x.reshape(M, D), g1).reshape(B, S, D)

        h = _norm_mod(x, norm2_g[l], sc2, sh2, BF16).reshape(M, D)
        act = _gate_up(h, w_gate[l], w_up[l])
        x = _down(act, w_down[l].astype(BF16), x.reshape(M, D), g2).reshape(B, S, D)
    zeros = jnp.zeros((B, D), F32)
    return _norm_mod(x, final_g, zeros, zeros, x.dtype)
```

```python
import functools
import math

import jax
import jax.numpy as jnp
from jax import lax
from jax.experimental import pallas as pl
from jax.experimental.pallas import tpu as pltpu

F32 = jnp.float32
BF16 = jnp.bfloat16

HEAD_DIM = 128
N_REC_HEADS = 16
CONV_WIDTH = 4
LRU_C = 8.0
DILATED_PATTERNS = ((128, 1), (512, 4), (2048, 16))
BLOCK = 128
REL_BUCKETS = 32
REL_MAX_DISTANCE = 2048
N_ADA = 6
EPS = 1e-6
NEG_INF = -1e30

SUBLANES = 8
VMEM_LIMIT = 56 * 1024 * 1024


def _params(*sem):
    return pltpu.CompilerParams(dimension_semantics=sem, vmem_limit_bytes=VMEM_LIMIT)


def _ada_kernel(c_ref, w_ref, b_ref, o_ref):
    c = c_ref[...]
    cond = (c * jax.nn.sigmoid(c)).astype(BF16)
    o_ref[...] = jnp.dot(cond, w_ref[...].astype(BF16), preferred_element_type=F32) + b_ref[...]


def _ada_mod(c, ada_w, ada_b, tn=512):
    B, D = c.shape
    N = ada_w.shape[1]
    rows = -(-B // SUBLANES) * SUBLANES
    c_pad = jnp.pad(c, ((0, rows - B), (0, 0)))
    out = pl.pallas_call(
        _ada_kernel,
        grid=(N // tn,),
        in_specs=[pl.BlockSpec((rows, D), lambda j: (0, 0)),
                  pl.BlockSpec((D, tn), lambda j: (0, j)),
                  pl.BlockSpec((1, tn), lambda j: (0, j))],
        out_specs=pl.BlockSpec((rows, tn), lambda j: (0, j)),
        out_shape=jax.ShapeDtypeStruct((rows, N), F32),
        compiler_params=_params("arbitrary"),
        name="ada_mod",
    )(c_pad, ada_w, ada_b.reshape(1, N))
    return out[:B]


def _norm_rows(x, g, sc, sh):
    ms = jnp.mean(x * x, axis=-1, keepdims=True)
    y = x * lax.rsqrt(ms + EPS) * g
    return y * (1.0 + sc) + sh


def _norm_kernel(x_ref, g_ref, sc_ref, sh_ref, o_ref):
    o_ref[...] = _norm_rows(x_ref[...], g_ref[...], sc_ref[...], sh_ref[...]).astype(o_ref.dtype)


def _norm_mod(x, g, sc, sh, out_dtype, ts=512, head_rows=None):
    B, S, D = x.shape
    nb, rows = (B, S) if head_rows is None else (1, head_rows)
    return pl.pallas_call(
        _norm_kernel,
        grid=(nb, rows // ts),
        in_specs=[pl.BlockSpec((None, ts, D), lambda b, t: (b, t, 0)),
                  pl.BlockSpec((1, D), lambda b, t: (0, 0)),
                  pl.BlockSpec((None, 1, D), lambda b, t: (b, 0, 0)),
                  pl.BlockSpec((None, 1, D), lambda b, t: (b, 0, 0))],
        out_specs=pl.BlockSpec((None, ts, D), lambda b, t: (b, t, 0)),
        out_shape=jax.ShapeDtypeStruct((nb, rows, D), out_dtype),
        compiler_params=_params("arbitrary", "arbitrary"),
        name="norm_mod",
    )(x, g.reshape(1, D), sc.reshape(B, 1, D), sh.reshape(B, 1, D))


def _normed_lhs(x2d, g, sc, sh, tm, n_slices, seq_len):
    M, D = x2d.shape
    B = sc.shape[0]
    n_tiles = M // tm
    rows = tm // n_slices
    h0 = _norm_mod(x2d.reshape(B, seq_len, D), g, sc, sh, BF16, head_rows=tm).reshape(tm, D)

    def slice_idx(i, j):
        return (jnp.minimum(i + 1, n_tiles - 1) * n_slices + jnp.minimum(j, n_slices - 1), 0)

    def batch_idx(i, j):
        return (jnp.minimum(i + 1, n_tiles - 1) * tm // seq_len, 0, 0)

    operands = (x2d, g.reshape(1, D), sc.reshape(B, 1, D), sh.reshape(B, 1, D), h0)
    specs = [pl.BlockSpec((rows, D), slice_idx),
             pl.BlockSpec((1, D), lambda i, j: (0, 0)),
             pl.BlockSpec((None, 1, D), batch_idx),
             pl.BlockSpec((None, 1, D), batch_idx),
             pl.BlockSpec(memory_space=pl.ANY)]
    scratch = [pltpu.VMEM((tm, D), BF16), pltpu.VMEM((tm, D), BF16)]
    return operands, specs, scratch


def _with_normed_lhs(step, xs_ref, g_ref, sc_ref, sh_ref, h0_ref, h_a, h_b, n_slices):
    i, j = pl.program_id(0), pl.program_id(1)
    rows = xs_ref.shape[0]

    @pl.when((i == 0) & (j == 0))
    def _():
        pltpu.sync_copy(h0_ref, h_a)

    def run(cur, nxt):
        step(cur)
        dst = pl.ds(pl.multiple_of(jnp.minimum(j, n_slices - 1) * rows, rows), rows)
        nxt[dst, :] = _norm_rows(xs_ref[...], g_ref[...], sc_ref[...], sh_ref[...]).astype(BF16)

    @pl.when(i % 2 == 0)
    def _():
        run(h_a, h_b)

    @pl.when(i % 2 == 1)
    def _():
        run(h_b, h_a)


def _inproj_kernel(xs_ref, g_ref, sc_ref, sh_ref, h0_ref, w_ref, rec_ref, qkv_ref, h_a, h_b, *, n_slices):
    def step(h_ref):
        acc = jnp.dot(h_ref[...], w_ref[...].astype(BF16), preferred_element_type=F32)
        rec_ref[...] = acc.astype(rec_ref.dtype)
        qkv_ref[...] = acc

    _with_normed_lhs(step, xs_ref, g_ref, sc_ref, sh_ref, h0_ref, h_a, h_b, n_slices)


INPROJ_TN = 512


def _in_proj(x, g, sc, sh, w_in, n_rec_cols, tm=1024, tn=INPROJ_TN, n_slices=16):
    B, S, K = x.shape
    M = B * S
    N = w_in.shape[1]
    n_rec = n_rec_cols // tn
    assert N // tn >= n_slices and S % tm == 0
    operands, specs, scratch = _normed_lhs(x.reshape(M, K), g, sc, sh, tm, n_slices, S)
    return pl.pallas_call(
        functools.partial(_inproj_kernel, n_slices=n_slices),
        grid=(M // tm, N // tn),
        in_specs=specs + [pl.BlockSpec((K, tn), lambda i, j: (0, j))],
        out_specs=[pl.BlockSpec((tm, tn), lambda i, j: (i, jnp.minimum(j, n_rec))),
                   pl.BlockSpec((tm, tn), lambda i, j: (i, jnp.maximum(j - n_rec + 1, 0)))],
        out_shape=[jax.ShapeDtypeStruct((M, n_rec_cols + tn), BF16),
                   jax.ShapeDtypeStruct((M, tn + N - n_rec_cols), F32)],
        scratch_shapes=scratch,
        compiler_params=_params("arbitrary", "arbitrary"),
        name="in_proj",
    )(*operands, w_in)


def _rglru_kernel(xr_ref, yg_ref, cw_ref, cb_ref, wai_ref, ba_ref, bi_ref, lam_ref, gn_ref,
                  o_ref, xbuf, a_s, u_s, h_s, hc, *, ts, pitch):
    nh = xbuf.shape[0]
    hd = xbuf.shape[2]
    t_blk = pl.program_id(1)

    @pl.when(t_blk == 0)
    def _():
        xbuf[:, 0:SUBLANES, :] = jnp.zeros((nh, SUBLANES, hd), F32)
        hc[...] = jnp.zeros(hc.shape, F32)

    for c in range(nh):
        lanes = slice(c * hd, (c + 1) * hd)
        xbuf[c, SUBLANES:SUBLANES + ts, :] = xr_ref[:, lanes].astype(F32)
        y = cb_ref[:, lanes] + cw_ref[CONV_WIDTH - 1:CONV_WIDTH, lanes] * xbuf[c, SUBLANES:SUBLANES + ts, :]
        for j in range(CONV_WIDTH - 1):
            off = SUBLANES - (CONV_WIDTH - 1) + j
            y = y + cw_ref[j:j + 1, lanes] * xbuf[c, off:off + ts, :]
        xbuf[c, 0:SUBLANES, :] = xbuf[c, ts:ts + SUBLANES, :]
        g = jnp.dot(y.astype(BF16), wai_ref[c], preferred_element_type=F32)
        r = jax.nn.sigmoid(g[:, :hd] + ba_ref[:, lanes])
        i = jax.nn.sigmoid(g[:, hd:] + bi_ref[:, lanes])
        z = -lam_ref[:, lanes]
        softplus = jnp.maximum(z, 0.0) + jnp.log1p(jnp.exp(-jnp.abs(z)))
        log_a = r * ((-LRU_C) * softplus)
        a = jnp.exp(log_a)
        a_s[c * pitch:c * pitch + ts, :] = a
        v = -jnp.tanh(log_a) * (a * a + 1.0)
        u_s[c * pitch:c * pitch + ts, :] = jnp.where(v > 0.0, v * lax.rsqrt(v), 0.0) * (i * y)

    ngrp = nh // SUBLANES

    def step(t, hs):
        new = []
        for gi in range(ngrp):
            rows = pl.ds(gi * SUBLANES * pitch + t, SUBLANES, stride=pitch)
            h = a_s[rows, :] * hs[gi] + u_s[rows, :]
            h_s[rows, :] = h
            new.append(h)
        return tuple(new)

    hs = lax.fori_loop(0, ts, step, tuple(hc[gi] for gi in range(ngrp)), unroll=8)
    for gi in range(ngrp):
        hc[gi] = hs[gi]

    ssq = jnp.zeros((ts, hd), F32)
    for c in range(nh):
        lanes = slice(c * hd, (c + 1) * hd)
        rec = h_s[c * pitch:c * pitch + ts, :] * jax.nn.gelu(yg_ref[:, lanes].astype(F32))
        a_s[c * pitch:c * pitch + ts, :] = rec
        ssq = ssq + rec * rec
    inv = lax.rsqrt(jnp.sum(ssq, axis=-1, keepdims=True) / (nh * hd) + EPS)
    for c in range(nh):
        lanes = slice(c * hd, (c + 1) * hd)
        o_ref[:, lanes] = (a_s[c * pitch:c * pitch + ts, :] * inv * gn_ref[:, lanes]).astype(o_ref.dtype)


def _rglru(rec_in, conv_w, conv_b, w_a, b_a, w_i, b_i, lam, gn, ts=256):
    B, S, _ = rec_in.shape
    R = conv_w.shape[-1]
    nh, hd = w_a.shape[0], w_a.shape[1]
    pitch = ts + SUBLANES
    wai = jnp.concatenate([w_a, w_i], axis=-1).astype(BF16)
    row = lambda v: v.reshape(1, R)
    vec = pl.BlockSpec((1, R), lambda b, t: (0, 0))
    return pl.pallas_call(
        functools.partial(_rglru_kernel, ts=ts, pitch=pitch),
        grid=(B, S // ts),
        in_specs=[pl.BlockSpec((None, ts, R), lambda b, t: (b, t, 0)),
                  pl.BlockSpec((None, ts, R), lambda b, t: (b, t, 1)),
                  pl.BlockSpec((CONV_WIDTH, R), lambda b, t: (0, 0)),
                  vec,
                  pl.BlockSpec((nh, hd, 2 * hd), lambda b, t: (0, 0, 0)),
                  vec, vec, vec, vec],
        out_specs=pl.BlockSpec((None, ts, R), lambda b, t: (b, t, 0)),
        out_shape=jax.ShapeDtypeStruct((B, S, R), BF16),
        scratch_shapes=[pltpu.VMEM((nh, ts + SUBLANES, hd), F32),
                        pltpu.VMEM((nh * pitch, hd), F32),
                        pltpu.VMEM((nh * pitch, hd), F32),
                        pltpu.VMEM((nh * pitch, hd), F32),
                        pltpu.VMEM((nh // SUBLANES, SUBLANES, hd), F32)],
        compiler_params=_params("arbitrary", "arbitrary"),
        name="rglru",
    )(rec_in, rec_in, conv_w, row(conv_b), wai, row(b_a), row(b_i), row(lam), row(gn))


def _t5_bucket(n):
    max_exact = REL_BUCKETS // 2
    nf = jnp.maximum(n, 1).astype(F32)
    large = max_exact + (jnp.log(nf / max_exact) / math.log(REL_MAX_DISTANCE / max_exact)
                         * (REL_BUCKETS - max_exact)).astype(jnp.int32)
    large = jnp.minimum(large, REL_BUCKETS - 1)
    return jnp.where(n < max_exact, n, large)


def _bias_kernel(rb_ref, bucket_ref, o_ref):
    h = pl.program_id(0)
    bucket = bucket_ref[...]
    bias = jnp.full(bucket.shape, NEG_INF, F32)
    for b in range(REL_BUCKETS):
        bias = jnp.where(bucket == b, rb_ref[h, b], bias)
    o_ref[...] = bias


def _attn_bias(rel_bias):
    n_heads = rel_bias.shape[1]
    qi = jnp.arange(BLOCK, dtype=jnp.int32)[:, None]
    kj = jnp.arange(2 * BLOCK, dtype=jnp.int32)[None, :]
    dist = qi + BLOCK - kj
    buckets = []
    for window, dil in DILATED_PATTERNS:
        band = (dist >= 0) & (dist <= window // dil)
        bucket = jnp.where(band, _t5_bucket(jnp.maximum(dist, 0) * dil), -1)
        buckets.append(jnp.stack([bucket, jnp.where(kj >= BLOCK, bucket, -1)]))
    buckets = jnp.stack(buckets)
    shape = buckets.shape
    return pl.pallas_call(
        _bias_kernel,
        grid=(n_heads,),
        in_specs=[pl.BlockSpec(memory_space=pltpu.SMEM),
                  pl.BlockSpec(shape, lambda h: (0, 0, 0, 0))],
        out_specs=pl.BlockSpec((None,) + shape, lambda h: (h, 0, 0, 0, 0)),
        out_shape=jax.ShapeDtypeStruct((n_heads,) + shape, F32),
        compiler_params=_params("arbitrary"),
        name="attn_bias",
    )(rel_bias.astype(F32).T, buckets)


ATTN_COPY_ROWS = 256
ATTN_MERGE_ROWS = 128


def _attn_kernel(q_ref, k_ref, v_ref, bias_ref, o_ref, x4, qs0, ks0, vs0, qs1, ks1, vs1, ob, lb,
                 *, S, unroll):
    scale = HEAD_DIM ** -0.5
    S4, S16 = S // 4, S // 16
    CH = ATTN_COPY_ROWS
    n_it = S // BLOCK // unroll
    srcs = (q_ref, k_ref, v_ref)
    offs = (0, BLOCK, BLOCK)
    muls = (scale, None, None)
    qs, ks, vs = (qs0, qs1), (ks0, ks1), (vs0, vs1)
    for st in range(2):
        ks[st][0:BLOCK, :] = jnp.zeros((BLOCK, HEAD_DIM), BF16)
        vs[st][0:BLOCK, :] = jnp.zeros((BLOCK, HEAD_DIM), BF16)

    def blocks(p, extra=None):
        nbr = (S // DILATED_PATTERNS[p][1]) // BLOCK
        st = p % 2

        def body(it, carry):
            ns = [it * unroll + u for u in range(unroll)]
            row0s = [pl.multiple_of(n * BLOCK, BLOCK) for n in ns]
            ss = []
            for n, row0 in zip(ns, row0s):
                q = qs[st][pl.ds(row0, BLOCK), :]
                kw = ks[st][pl.ds(row0, 2 * BLOCK), :]
                ss.append(lax.dot_general(q, kw, (((1,), (1,)), ((), ())), preferred_element_type=F32))
            if extra is not None:
                extra(it)
            es, stats = [], []
            for n, s in zip(ns, ss):
                s = s + bias_ref[p, jnp.where(n % nbr == 0, 1, 0)]
                m = jnp.max(s, axis=-1, keepdims=True)
                e = jnp.exp(s - m)
                es.append(e.astype(BF16))
                stats.append((m, jnp.sum(e, axis=-1, keepdims=True)))
            for n, row0, e, (m, den) in zip(ns, row0s, es, stats):
                vw = vs[st][pl.ds(row0, 2 * BLOCK), :]
                o = jnp.dot(e, vw, preferred_element_type=F32) / den
                lse = jnp.broadcast_to(m + jnp.log(den), (BLOCK, HEAD_DIM))
                if p < 2:
                    rows = pl.ds(row0, BLOCK)
                else:
                    r = n // nbr
                    rows = pl.ds((r % 4) * S4 + 4 * (n % nbr) * BLOCK + r // 4, BLOCK, stride=4)
                ob[p, rows, :] = o
                lb[p, rows, :] = lse
            return carry

        lax.fori_loop(0, n_it, body, 0)

    dsts = (qs, ks, vs)

    def copy1(i, carry):
        r0 = pl.multiple_of(i * CH, CH)
        for src, dst, off, mul in zip(srcs, dsts, offs, muls):
            x = src[pl.ds(r0, CH), :]
            dst[0][pl.ds(off + r0, CH), :] = (x if mul is None else x * mul).astype(BF16)
        return carry

    lax.fori_loop(0, S // CH, copy1, 0)

    n4 = S4 // CH

    def copy4(i):
        r, c = i // n4, i % n4
        r0 = pl.multiple_of(r * S4 + c * CH, CH)
        for t, (src, dst, off, mul) in enumerate(zip(srcs, dsts, offs, muls)):
            x = src[pl.ds(r + 4 * c * CH, CH, stride=4), :]
            x = x if mul is None else x * mul
            x4[t, pl.ds(r0, CH), :] = x
            dst[1][pl.ds(off + r0, CH), :] = x.astype(BF16)

    n16 = S16 // CH

    def copy16(i):
        r, c = i // n16, i % n16
        r0 = pl.multiple_of(r * S16 + c * CH, CH)
        for t, (dst, off) in enumerate(zip(dsts, offs)):
            x = x4[t, pl.ds((r % 4) * S4 + r // 4 + 4 * c * CH, CH, stride=4), :]
            dst[0][pl.ds(off + r0, CH), :] = x.astype(BF16)

    def spread(copy, total):
        per_it = total // n_it
        return lambda it: [copy(it * per_it + s) for s in range(per_it)]

    blocks(0, spread(copy4, 4 * n4))
    blocks(1, spread(copy16, 16 * n16))
    blocks(2)

    CM = ATTN_MERGE_ROWS
    nm = S4 // CM

    def merge(i, carry):
        r, c = i // nm, i % nm
        tok = pl.ds(r + 4 * c * CM, CM, stride=4)
        run = pl.ds(pl.multiple_of(r * S4 + c * CM, CM), CM)
        ls = [lb[0, tok, :], lb[1, run, :], lb[2, run, :]]
        os_ = [ob[0, tok, :], ob[1, run, :], ob[2, run, :]]
        mx = jnp.maximum(jnp.maximum(ls[0], ls[1]), ls[2])
        ws = [jnp.exp(l - mx) for l in ls]
        num = ws[0] * os_[0] + ws[1] * os_[1] + ws[2] * os_[2]
        o_ref[tok, :] = num / (ws[0] + ws[1] + ws[2])
        return carry

    lax.fori_loop(0, 4 * nm, merge, 0)


def _dilated_attention(qkv, rel_bias, n_heads, col0=0, unroll=16):
    B, S, _ = qkv.shape
    A = n_heads * HEAD_DIM
    blk0 = col0 // HEAD_DIM
    assert tuple(d for _, d in DILATED_PATTERNS) == (1, 4, 16)
    assert all(w // d == BLOCK for w, d in DILATED_PATTERNS)
    assert S % (16 * ATTN_COPY_ROWS) == 0 and S % (BLOCK * unroll) == 0
    bias = _attn_bias(rel_bias)
    npat = len(DILATED_PATTERNS)
    head = lambda off: pl.BlockSpec((None, S, HEAD_DIM), lambda b, h: (b, 0, blk0 + off + h))
    return pl.pallas_call(
        functools.partial(_attn_kernel, S=S, unroll=unroll),
        grid=(B, n_heads),
        in_specs=[head(0), head(n_heads), head(2 * n_heads),
                  pl.BlockSpec((None, npat, 2, BLOCK, 2 * BLOCK), lambda b, h: (h, 0, 0, 0, 0))],
        out_specs=pl.BlockSpec((None, S, HEAD_DIM), lambda b, h: (b, 0, h)),
        out_shape=jax.ShapeDtypeStruct((B, S, A), F32),
        scratch_shapes=[pltpu.VMEM((3, S, HEAD_DIM), F32),
                        pltpu.VMEM((S, HEAD_DIM), BF16),
                        pltpu.VMEM((S + BLOCK, HEAD_DIM), BF16),
                        pltpu.VMEM((S + BLOCK, HEAD_DIM), BF16),
                        pltpu.VMEM((S, HEAD_DIM), BF16),
                        pltpu.VMEM((S + BLOCK, HEAD_DIM), BF16),
                        pltpu.VMEM((S + BLOCK, HEAD_DIM), BF16),
                        pltpu.VMEM((npat, S, HEAD_DIM), F32),
                        pltpu.VMEM((npat, S, HEAD_DIM), F32)],
        compiler_params=_params("arbitrary", "arbitrary"),
        name="dilated_attn",
    )(qkv, qkv, qkv, bias)


def _outproj_kernel(rec_ref, att_ref, gn_ref, w1_ref, w2_ref, x_ref, g_ref, o_ref, att_n):
    @pl.when(pl.program_id(1) == 0)
    def _():
        a = att_ref[...].astype(F32)
        ms = jnp.mean(a * a, axis=-1, keepdims=True)
        att_n[...] = (a * lax.rsqrt(ms + EPS) * gn_ref[...]).astype(BF16)

    acc = jnp.dot(rec_ref[...], w1_ref[...], preferred_element_type=F32)
    acc = acc + jnp.dot(att_n[...], w2_ref[...], preferred_element_type=F32)
    o_ref[...] = x_ref[...] + g_ref[...] * acc


def _out_proj(rec_n, att, gn_att, w_out, x, gate, tm=1024, tn=512):
    M, R = rec_n.shape
    A = att.shape[1]
    D = w_out.shape[1]
    B = gate.shape[0]
    S = M // B
    return pl.pallas_call(
        _outproj_kernel,
        grid=(M // tm, D // tn),
        in_specs=[pl.BlockSpec((tm, R), lambda i, j: (i, 0)),
                  pl.BlockSpec((tm, A), lambda i, j: (i, 0)),
                  pl.BlockSpec((1, A), lambda i, j: (0, 0)),
                  pl.BlockSpec((R, tn), lambda i, j: (0, j)),
                  pl.BlockSpec((A, tn), lambda i, j: (R // A, j)),
                  pl.BlockSpec((tm, tn), lambda i, j: (i, j)),
                  pl.BlockSpec((None, 1, tn), lambda i, j: (i * tm // S, 0, j))],
        out_specs=pl.BlockSpec((tm, tn), lambda i, j: (i, j)),
        out_shape=jax.ShapeDtypeStruct((M, D), F32),
        scratch_shapes=[pltpu.VMEM((tm, A), BF16)],
        compiler_params=_params("arbitrary", "arbitrary"),
        name="out_proj",
    )(rec_n, att, gn_att.reshape(1, A), w_out, w_out, x, gate.reshape(B, 1, D))


def _gateup_kernel(h_ref, wg_ref, wu_ref, o_ref):
    h = h_ref[...]
    g = jnp.dot(h, wg_ref[...].astype(BF16), preferred_element_type=F32)
    u = jnp.dot(h, wu_ref[...].astype(BF16), preferred_element_type=F32)
    o_ref[...] = (g * jax.nn.sigmoid(g) * u).astype(o_ref.dtype)


def _gate_up(h, w_gate, w_up, tm=1024, tf=256):
    M, D = h.shape
    F = w_gate.shape[1]
    return pl.pallas_call(
        _gateup_kernel,
        grid=(M // tm, F // tf),
        in_specs=[pl.BlockSpec((tm, D), lambda i, j: (i, 0)),
                  pl.BlockSpec((D, tf), lambda i, j: (0, j)),
                  pl.BlockSpec((D, tf), lambda i, j: (0, j))],
        out_specs=pl.BlockSpec((tm, tf), lambda i, j: (i, j)),
        out_shape=jax.ShapeDtypeStruct((M, F), BF16),
        compiler_params=_params("arbitrary", "arbitrary"),
        name="gate_up",
    )(h, w_gate, w_up)


def _down_kernel(a_ref, w_ref, x_ref, g_ref, o_ref):
    acc = jnp.dot(a_ref[...], w_ref[...], preferred_element_type=F32)
    o_ref[...] = x_ref[...] + g_ref[...] * acc


def _down(act, w_down, x, gate, tm=512, tn=512):
    M, F = act.shape
    D = w_down.shape[1]
    B = gate.shape[0]
    S = M // B
    return pl.pallas_call(
        _down_kernel,
        grid=(M // tm, D // tn),
        in_specs=[pl.BlockSpec((tm, F), lambda i, j: (i, 0)),
                  pl.BlockSpec((F, tn), lambda i, j: (0, j)),
                  pl.BlockSpec((tm, tn), lambda i, j: (i, j)),
                  pl.BlockSpec((None, 1, tn), lambda i, j: (i * tm // S, 0, j))],
        out_specs=pl.BlockSpec((tm, tn), lambda i, j: (i, j)),
        out_shape=jax.ShapeDtypeStruct((M, D), F32),
        compiler_params=_params("arbitrary", "arbitrary"),
        name="down_proj",
    )(act, w_down, x, gate.reshape(B, 1, D))


def kernel(x, c, ada_w, ada_b, norm1_g, norm2_g, w_in, conv_w, conv_b, rg_w_a, rg_b_a, rg_w_i, rg_b_i,
           lru_lambda, rel_bias, gnorm_rec, gnorm_attn, w_out, w_gate, w_up, w_down, final_g):
    B, S, D = x.shape
    depth = ada_w.shape[0]
    R = conv_w.shape[-1]
    A = gnorm_attn.shape[-1]
    n_heads = A // HEAD_DIM
    M = B * S
    for l in range(depth):
        mod = _ada_mod(c, ada_w[l], ada_b[l])
        sh1, sc1, g1, sh2, sc2, g2 = jnp.split(mod, N_ADA, axis=-1)

        rec_in, qkv = _in_proj(x, norm1_g[l], sc1, sh1, w_in[l], 2 * R)
        rec_in = rec_in.reshape(B, S, rec_in.shape[-1])
        qkv = qkv.reshape(B, S, qkv.shape[-1])
        rec_n = _rglru(rec_in, conv_w[l], conv_b[l], rg_w_a[l], rg_b_a[l], rg_w_i[l], rg_b_i[l],
                       lru_lambda[l], gnorm_rec[l])
        att = _dilated_attention(qkv, rel_bias, n_heads, col0=INPROJ_TN)
        x = _out_proj(rec_n.reshape(M, R), att.reshape(M, A), gnorm_attn[l], w_out[l].astype(BF16),
                      x.reshape(M, D), g1).reshape(B, S, D)

        h = _norm_mod(x, norm2_g[l], sc2, sh2, BF16).reshape(M, D)
        act = _gate_up(h, w_gate[l], w_up[l])
        x = _down(act, w_down[l].astype(BF16), x.reshape(M, D), g2).reshape(B, S, D)
    zeros = jnp.zeros((B, D), F32)
    return _norm_mod(x, final_g, zeros, zeros, x.dtype)
```

```python
import functools
import math

import jax
import jax.numpy as jnp
from jax import lax
from jax.experimental import pallas as pl
from jax.experimental.pallas import tpu as pltpu

F32 = jnp.float32
BF16 = jnp.bfloat16

HEAD_DIM = 128
N_REC_HEADS = 16
CONV_WIDTH = 4
LRU_C = 8.0
DILATED_PATTERNS = ((128, 1), (512, 4), (2048, 16))
BLOCK = 128
REL_BUCKETS = 32
REL_MAX_DISTANCE = 2048
N_ADA = 6
EPS = 1e-6
NEG_INF = -1e30

SUBLANES = 8
VMEM_LIMIT = 56 * 1024 * 1024


def _params(*sem):
    return pltpu.CompilerParams(dimension_semantics=sem, vmem_limit_bytes=VMEM_LIMIT)


def _ada_kernel(c_ref, w_ref, b_ref, o_ref):
    c = c_ref[...]
    cond = (c * jax.nn.sigmoid(c)).astype(BF16)
    o_ref[...] = jnp.dot(cond, w_ref[...].astype(BF16), preferred_element_type=F32) + b_ref[...]


def _ada_mod(c, ada_w, ada_b, tn=512):
    B, D = c.shape
    N = ada_w.shape[1]
    rows = -(-B // SUBLANES) * SUBLANES
    c_pad = jnp.pad(c, ((0, rows - B), (0, 0)))
    out = pl.pallas_call(
        _ada_kernel,
        grid=(N // tn,),
        in_specs=[pl.BlockSpec((rows, D), lambda j: (0, 0)),
                  pl.BlockSpec((D, tn), lambda j: (0, j)),
                  pl.BlockSpec((1, tn), lambda j: (0, j))],
        out_specs=pl.BlockSpec((rows, tn), lambda j: (0, j)),
        out_shape=jax.ShapeDtypeStruct((rows, N), F32),
        compiler_params=_params("arbitrary"),
        name="ada_mod",
    )(c_pad, ada_w, ada_b.reshape(1, N))
    return out[:B]


def _norm_rows(x, g, sc, sh):
    ms = jnp.mean(x * x, axis=-1, keepdims=True)
    y = x * lax.rsqrt(ms + EPS) * g
    return y * (1.0 + sc) + sh


def _norm_kernel(x_ref, g_ref, sc_ref, sh_ref, o_ref):
    o_ref[...] = _norm_rows(x_ref[...], g_ref[...], sc_ref[...], sh_ref[...]).astype(o_ref.dtype)


def _norm_mod(x, g, sc, sh, out_dtype, ts=512, head_rows=None):
    B, S, D = x.shape
    nb, rows = (B, S) if head_rows is None else (1, head_rows)
    return pl.pallas_call(
        _norm_kernel,
        grid=(nb, rows // ts),
        in_specs=[pl.BlockSpec((None, ts, D), lambda b, t: (b, t, 0)),
                  pl.BlockSpec((1, D), lambda b, t: (0, 0)),
                  pl.BlockSpec((None, 1, D), lambda b, t: (b, 0, 0)),
                  pl.BlockSpec((None, 1, D), lambda b, t: (b, 0, 0))],
        out_specs=pl.BlockSpec((None, ts, D), lambda b, t: (b, t, 0)),
        out_shape=jax.ShapeDtypeStruct((nb, rows, D), out_dtype),
        compiler_params=_params("arbitrary", "arbitrary"),
        name="norm_mod",
    )(x, g.reshape(1, D), sc.reshape(B, 1, D), sh.reshape(B, 1, D))


def _normed_lhs(x2d, g, sc, sh, tm, n_slices, seq_len):
    M, D = x2d.shape
    B = sc.shape[0]
    n_tiles = M // tm
    rows = tm // n_slices
    h0 = _norm_mod(x2d.reshape(B, seq_len, D), g, sc, sh, BF16, head_rows=tm).reshape(tm, D)

    def slice_idx(i, j):
        return (jnp.minimum(i + 1, n_tiles - 1) * n_slices + jnp.minimum(j, n_slices - 1), 0)

    def batch_idx(i, j):
        return (jnp.minimum(i + 1, n_tiles - 1) * tm // seq_len, 0, 0)

    operands = (x2d, g.reshape(1, D), sc.reshape(B, 1, D), sh.reshape(B, 1, D), h0)
    specs = [pl.BlockSpec((rows, D), slice_idx),
             pl.BlockSpec((1, D), lambda i, j: (0, 0)),
             pl.BlockSpec((None, 1, D), batch_idx),
             pl.BlockSpec((None, 1, D), batch_idx),
             pl.BlockSpec(memory_space=pl.ANY)]
    scratch = [pltpu.VMEM((tm, D), BF16), pltpu.VMEM((tm, D), BF16)]
    return operands, specs, scratch


def _with_normed_lhs(step, xs_ref, g_ref, sc_ref, sh_ref, h0_ref, h_a, h_b, n_slices):
    i, j = pl.program_id(0), pl.program_id(1)
    rows = xs_ref.shape[0]

    @pl.when((i == 0) & (j == 0))
    def _():
        pltpu.sync_copy(h0_ref, h_a)

    def run(cur, nxt):
        step(cur)
        dst = pl.ds(pl.multiple_of(jnp.minimum(j, n_slices - 1) * rows, rows), rows)
        nxt[dst, :] = _norm_rows(xs_ref[...], g_ref[...], sc_ref[...], sh_ref[...]).astype(BF16)

    @pl.when(i % 2 == 0)
    def _():
        run(h_a, h_b)

    @pl.when(i % 2 == 1)
    def _():
        run(h_b, h_a)


def _inproj_kernel(xs_ref, g_ref, sc_ref, sh_ref, h0_ref, w_ref, rec_ref, qkv_ref, h_a, h_b, *, n_slices):
    def step(h_ref):
        acc = jnp.dot(h_ref[...], w_ref[...].astype(BF16), preferred_element_type=F32)
        rec_ref[...] = acc.astype(rec_ref.dtype)
        qkv_ref[...] = acc

    _with_normed_lhs(step, xs_ref, g_ref, sc_ref, sh_ref, h0_ref, h_a, h_b, n_slices)


INPROJ_TN = 512


def _in_proj(x, g, sc, sh, w_in, n_rec_cols, tm=1024, tn=INPROJ_TN, n_slices=16):
    B, S, K = x.shape
    M = B * S
    N = w_in.shape[1]
    n_rec = n_rec_cols // tn
    assert N // tn >= n_slices and S % tm == 0
    operands, specs, scratch = _normed_lhs(x.reshape(M, K), g, sc, sh, tm, n_slices, S)
    return pl.pallas_call(
        functools.partial(_inproj_kernel, n_slices=n_slices),
        grid=(M // tm, N // tn),
        in_specs=specs + [pl.BlockSpec((K, tn), lambda i, j: (0, j))],
        out_specs=[pl.BlockSpec((tm, tn), lambda i, j: (i, jnp.minimum(j, n_rec))),
                   pl.BlockSpec((tm, tn), lambda i, j: (i, jnp.maximum(j - n_rec + 1, 0)))],
        out_shape=[jax.ShapeDtypeStruct((M, n_rec_cols + tn), BF16),
                   jax.ShapeDtypeStruct((M, tn + N - n_rec_cols), F32)],
        scratch_shapes=scratch,
        compiler_params=_params("arbitrary", "arbitrary"),
        name="in_proj",
    )(*operands, w_in)


def _rglru_kernel(xr_ref, yg_ref, cw_ref, cb_ref, wai_ref, ba_ref, bi_ref, lam_ref, gn_ref,
                  o_ref, xbuf, a_s, u_s, h_s, hc, *, ts, pitch):
    nh = xbuf.shape[0]
    hd = xbuf.shape[2]
    t_blk = pl.program_id(1)

    @pl.when(t_blk == 0)
    def _():
        xbuf[:, 0:SUBLANES, :] = jnp.zeros((nh, SUBLANES, hd), F32)
        hc[...] = jnp.zeros(hc.shape, F32)

    for c in range(nh):
        lanes = slice(c * hd, (c + 1) * hd)
        xbuf[c, SUBLANES:SUBLANES + ts, :] = xr_ref[:, lanes].astype(F32)
        y = cb_ref[:, lanes] + cw_ref[CONV_WIDTH - 1:CONV_WIDTH, lanes] * xbuf[c, SUBLANES:SUBLANES + ts, :]
        for j in range(CONV_WIDTH - 1):
            off = SUBLANES - (CONV_WIDTH - 1) + j
            y = y + cw_ref[j:j + 1, lanes] * xbuf[c, off:off + ts, :]
        xbuf[c, 0:SUBLANES, :] = xbuf[c, ts:ts + SUBLANES, :]
        g = jnp.dot(y.astype(BF16), wai_ref[c], preferred_element_type=F32)
        r = jax.nn.sigmoid(g[:, :hd] + ba_ref[:, lanes])
        i = jax.nn.sigmoid(g[:, hd:] + bi_ref[:, lanes])
        z = -lam_ref[:, lanes]
        softplus = jnp.maximum(z, 0.0) + jnp.log1p(jnp.exp(-jnp.abs(z)))
        log_a = r * ((-LRU_C) * softplus)
        a = jnp.exp(log_a)
        a_s[c * pitch:c * pitch + ts, :] = a
        v = -jnp.tanh(log_a) * (a * a + 1.0)
        u_s[c * pitch:c * pitch + ts, :] = jnp.where(v > 0.0, v * lax.rsqrt(v), 0.0) * (i * y)

    ngrp = nh // SUBLANES

    def step(t, hs):
        new = []
        for gi in range(ngrp):
            rows = pl.ds(gi * SUBLANES * pitch + t, SUBLANES, stride=pitch)
            h = a_s[rows, :] * hs[gi] + u_s[rows, :]
            h_s[rows, :] = h
            new.append(h)
        return tuple(new)

    hs = lax.fori_loop(0, ts, step, tuple(hc[gi] for gi in range(ngrp)), unroll=8)
    for gi in range(ngrp):
        hc[gi] = hs[gi]

    ssq = jnp.zeros((ts, hd), F32)
    for c in range(nh):
        lanes = slice(c * hd, (c + 1) * hd)
        rec = h_s[c * pitch:c * pitch + ts, :] * jax.nn.gelu(yg_ref[:, lanes].astype(F32))
        a_s[c * pitch:c * pitch + ts, :] = rec
        ssq = ssq + rec * rec
    inv = lax.rsqrt(jnp.sum(ssq, axis=-1, keepdims=True) / (nh * hd) + EPS)
    for c in range(nh):
        lanes = slice(c * hd, (c + 1) * hd)
        o_ref[:, lanes] = (a_s[c * pitch:c * pitch + ts, :] * inv * gn_ref[:, lanes]).astype(o_ref.dtype)


def _rglru(rec_in, conv_w, conv_b, w_a, b_a, w_i, b_i, lam, gn, ts=256):
    B, S, _ = rec_in.shape
    R = conv_w.shape[-1]
    nh, hd = w_a.shape[0], w_a.shape[1]
    pitch = ts + SUBLANES
    wai = jnp.concatenate([w_a, w_i], axis=-1).astype(BF16)
    row = lambda v: v.reshape(1, R)
    vec = pl.BlockSpec((1, R), lambda b, t: (0, 0))
    return pl.pallas_call(
        functools.partial(_rglru_kernel, ts=ts, pitch=pitch),
        grid=(B, S // ts),
        in_specs=[pl.BlockSpec((None, ts, R), lambda b, t: (b, t, 0)),
                  pl.BlockSpec((None, ts, R), lambda b, t: (b, t, 1)),
                  pl.BlockSpec((CONV_WIDTH, R), lambda b, t: (0, 0)),
                  vec,
                  pl.BlockSpec((nh, hd, 2 * hd), lambda b, t: (0, 0, 0)),
                  vec, vec, vec, vec],
        out_specs=pl.BlockSpec((None, ts, R), lambda b, t: (b, t, 0)),
        out_shape=jax.ShapeDtypeStruct((B, S, R), BF16),
        scratch_shapes=[pltpu.VMEM((nh, ts + SUBLANES, hd), F32),
                        pltpu.VMEM((nh * pitch, hd), F32),
                        pltpu.VMEM((nh * pitch, hd), F32),
                        pltpu.VMEM((nh * pitch, hd), F32),
                        pltpu.VMEM((nh // SUBLANES, SUBLANES, hd), F32)],
        compiler_params=_params("arbitrary", "arbitrary"),
        name="rglru",
    )(rec_in, rec_in, conv_w, row(conv_b), wai, row(b_a), row(b_i), row(lam), row(gn))


def _t5_bucket(n):
    max_exact = REL_BUCKETS // 2
    nf = jnp.maximum(n, 1).astype(F32)
    large = max_exact + (jnp.log(nf / max_exact) / math.log(REL_MAX_DISTANCE / max_exact)
                         * (REL_BUCKETS - max_exact)).astype(jnp.int32)
    large = jnp.minimum(large, REL_BUCKETS - 1)
    return jnp.where(n < max_exact, n, large)


def _bias_kernel(rb_ref, bucket_ref, o_ref):
    h = pl.program_id(0)
    bucket = bucket_ref[...]
    bias = jnp.full(bucket.shape, NEG_INF, F32)
    for b in range(REL_BUCKETS):
        bias = jnp.where(bucket == b, rb_ref[h, b], bias)
    o_ref[...] = bias


def _attn_bias(rel_bias):
    n_heads = rel_bias.shape[1]
    qi = jnp.arange(BLOCK, dtype=jnp.int32)[:, None]
    kj = jnp.arange(2 * BLOCK, dtype=jnp.int32)[None, :]
    dist = qi + BLOCK - kj
    buckets = []
    for window, dil in DILATED_PATTERNS:
        band = (dist >= 0) & (dist <= window // dil)
        bucket = jnp.where(band, _t5_bucket(jnp.maximum(dist, 0) * dil), -1)
        buckets.append(jnp.stack([bucket, jnp.where(kj >= BLOCK, bucket, -1)]))
    buckets = jnp.stack(buckets)
    shape = buckets.shape
    return pl.pallas_call(
        _bias_kernel,
        grid=(n_heads,),
        in_specs=[pl.BlockSpec(memory_space=pltpu.SMEM),
                  pl.BlockSpec(shape, lambda h: (0, 0, 0, 0))],
        out_specs=pl.BlockSpec((None,) + shape, lambda h: (h, 0, 0, 0, 0)),
        out_shape=jax.ShapeDtypeStruct((n_heads,) + shape, F32),
        compiler_params=_params("arbitrary"),
        name="attn_bias",
    )(rel_bias.astype(F32).T, buckets)


ATTN_COPY_ROWS = 256
ATTN_MERGE_ROWS = 128


def _attn_kernel(q_ref, k_ref, v_ref, bias_ref, o_ref, x4, qs0, ks0, vs0, qs1, ks1, vs1, ob, lb,
                 *, S, unroll):
    scale = HEAD_DIM ** -0.5
    S4, S16 = S // 4, S // 16
    CH = ATTN_COPY_ROWS
    n_it = S // BLOCK // unroll
    srcs = (q_ref, k_ref, v_ref)
    offs = (0, BLOCK, BLOCK)
    muls = (scale, None, None)
    qs, ks, vs = (qs0, qs1), (ks0, ks1), (vs0, vs1)
    for st in range(2):
        ks[st][0:BLOCK, :] = jnp.zeros((BLOCK, HEAD_DIM), BF16)
        vs[st][0:BLOCK, :] = jnp.zeros((BLOCK, HEAD_DIM), BF16)

    def blocks(p, extra=None):
        nbr = (S // DILATED_PATTERNS[p][1]) // BLOCK
        st = p % 2

        def body(it, carry):
            ns = [it * unroll + u for u in range(unroll)]
            row0s = [pl.multiple_of(n * BLOCK, BLOCK) for n in ns]
            ss = []
            for n, row0 in zip(ns, row0s):
                q = qs[st][pl.ds(row0, BLOCK), :]
                kw = ks[st][pl.ds(row0, 2 * BLOCK), :]
                ss.append(lax.dot_general(q, kw, (((1,), (1,)), ((), ())), preferred_element_type=F32))
            if extra is not None:
                extra(it)
            es, stats = [], []
            for n, s in zip(ns, ss):
                s = s + bias_ref[p, jnp.where(n % nbr == 0, 1, 0)]
                m = jnp.max(s, axis=-1, keepdims=True)
                e = jnp.exp(s - m)
                es.append(e.astype(BF16))
                stats.append((m, jnp.sum(e, axis=-1, keepdims=True)))
            for n, row0, e, (m, den) in zip(ns, row0s, es, stats):
                vw = vs[st][pl.ds(row0, 2 * BLOCK), :]
                o = jnp.dot(e, vw, preferred_element_type=F32) / den
                lse = jnp.broadcast_to(m + jnp.log(den), (BLOCK, HEAD_DIM))
                if p < 2:
                    rows = pl.ds(row0, BLOCK)
                else:
                    r = n // nbr
                    rows = pl.ds((r % 4) * S4 + 4 * (n % nbr) * BLOCK + r // 4, BLOCK, stride=4)
                ob[p, rows, :] = o
                lb[p, rows, :] = lse
            return carry

        lax.fori_loop(0, n_it, body, 0)

    dsts = (qs, ks, vs)

    def copy1(i, carry):
        r0 = pl.multiple_of(i * CH, CH)
        for src, dst, off, mul in zip(srcs, dsts, offs, muls):
            x = src[pl.ds(r0, CH), :]
            dst[0][pl.ds(off + r0, CH), :] = (x if mul is None else x * mul).astype(BF16)
        return carry

    lax.fori_loop(0, S // CH, copy1, 0)

    n4 = S4 // CH

    def copy4(i):
        r, c = i // n4, i % n4
        r0 = pl.multiple_of(r * S4 + c * CH, CH)
        for t, (src, dst, off, mul) in enumerate(zip(srcs, dsts, offs, muls)):
            x = src[pl.ds(r + 4 * c * CH, CH, stride=4), :]
            x = x if mul is None else x * mul
            x4[t, pl.ds(r0, CH), :] = x
            dst[1][pl.ds(off + r0, CH), :] = x.astype(BF16)

    n16 = S16 // CH

    def copy16(i):
        r, c = i // n16, i % n16
        r0 = pl.multiple_of(r * S16 + c * CH, CH)
        for t, (dst, off) in enumerate(zip(dsts, offs)):
            x = x4[t, pl.ds((r % 4) * S4 + r // 4 + 4 * c * CH, CH, stride=4), :]
            dst[0][pl.ds(off + r0, CH), :] = x.astype(BF16)

    def spread(copy, total):
        per_it = total // n_it
        return lambda it: [copy(it * per_it + s) for s in range(per_it)]

    blocks(0, spread(copy4, 4 * n4))
    blocks(1, spread(copy16, 16 * n16))
    blocks(2)

    CM = ATTN_MERGE_ROWS
    nm = S4 // CM

    def merge(i, carry):
        r, c = i // nm, i % nm
        tok = pl.ds(r + 4 * c * CM, CM, stride=4)
        run = pl.ds(pl.multiple_of(r * S4 + c * CM, CM), CM)
        ls = [lb[0, tok, :], lb[1, run, :], lb[2, run, :]]
        os_ = [ob[0, tok, :], ob[1, run, :], ob[2, run, :]]
        mx = jnp.maximum(jnp.maximum(ls[0], ls[1]), ls[2])
        ws = [jnp.exp(l - mx) for l in ls]
        num = ws[0] * os_[0] + ws[1] * os_[1] + ws[2] * os_[2]
        x4[0, tok, :] = num / (ws[0] + ws[1] + ws[2])
        return carry

    lax.fori_loop(0, 4 * nm, merge, 0)

    def emit(i, carry):
        rows = pl.ds(pl.multiple_of(i * CH, CH), CH)
        o_ref[rows, :] = x4[0, rows, :].astype(o_ref.dtype)
        return carry

    lax.fori_loop(0, S // CH, emit, 0)


def _dilated_attention(qkv, rel_bias, n_heads, col0=0, unroll=16):
    B, S, _ = qkv.shape
    A = n_heads * HEAD_DIM
    blk0 = col0 // HEAD_DIM
    assert tuple(d for _, d in DILATED_PATTERNS) == (1, 4, 16)
    assert all(w // d == BLOCK for w, d in DILATED_PATTERNS)
    assert S % (16 * ATTN_COPY_ROWS) == 0 and S % (BLOCK * unroll) == 0
    bias = _attn_bias(rel_bias)
    npat = len(DILATED_PATTERNS)
    head = lambda off: pl.BlockSpec((None, S, HEAD_DIM), lambda b, h: (b, 0, blk0 + off + h))
    return pl.pallas_call(
        functools.partial(_attn_kernel, S=S, unroll=unroll),
        grid=(B, n_heads),
        in_specs=[head(0), head(n_heads), head(2 * n_heads),
                  pl.BlockSpec((None, npat, 2, BLOCK, 2 * BLOCK), lambda b, h: (h, 0, 0, 0, 0))],
        out_specs=pl.BlockSpec((None, S, HEAD_DIM), lambda b, h: (b, 0, h)),
        out_shape=jax.ShapeDtypeStruct((B, S, A), BF16),
        scratch_shapes=[pltpu.VMEM((3, S, HEAD_DIM), F32),
                        pltpu.VMEM((S, HEAD_DIM), BF16),
                        pltpu.VMEM((S + BLOCK, HEAD_DIM), BF16),
                        pltpu.VMEM((S + BLOCK, HEAD_DIM), BF16),
                        pltpu.VMEM((S, HEAD_DIM), BF16),
                        pltpu.VMEM((S + BLOCK, HEAD_DIM), BF16),
                        pltpu.VMEM((S + BLOCK, HEAD_DIM), BF16),
                        pltpu.VMEM((npat, S, HEAD_DIM), F32),
                        pltpu.VMEM((npat, S, HEAD_DIM), F32)],
        compiler_params=_params("arbitrary", "arbitrary"),
        name="dilated_attn",
    )(qkv, qkv, qkv, bias)


def _outproj_kernel(rec_ref, att_ref, gn_ref, w1_ref, w2_ref, x_ref, g_ref, o_ref, att_n):
    @pl.when(pl.program_id(1) == 0)
    def _():
        a = att_ref[...].astype(F32)
        ms = jnp.mean(a * a, axis=-1, keepdims=True)
        att_n[...] = (a * lax.rsqrt(ms + EPS) * gn_ref[...]).astype(BF16)

    acc = jnp.dot(rec_ref[...], w1_ref[...].astype(BF16), preferred_element_type=F32)
    acc = acc + jnp.dot(att_n[...], w2_ref[...].astype(BF16), preferred_element_type=F32)
    o_ref[...] = x_ref[...] + g_ref[...] * acc


def _out_proj(rec_n, att, gn_att, w_out, x, gate, tm=1024, tn=512):
    M, R = rec_n.shape
    A = att.shape[1]
    D = w_out.shape[1]
    B = gate.shape[0]
    S = M // B
    return pl.pallas_call(
        _outproj_kernel,
        grid=(M // tm, D // tn),
        in_specs=[pl.BlockSpec((tm, R), lambda i, j: (i, 0)),
                  pl.BlockSpec((tm, A), lambda i, j: (i, 0)),
                  pl.BlockSpec((1, A), lambda i, j: (0, 0)),
                  pl.BlockSpec((R, tn), lambda i, j: (0, j)),
                  pl.BlockSpec((A, tn), lambda i, j: (R // A, j)),
                  pl.BlockSpec((tm, tn), lambda i, j: (i, j)),
                  pl.BlockSpec((None, 1, tn), lambda i, j: (i * tm // S, 0, j))],
        out_specs=pl.BlockSpec((tm, tn), lambda i, j: (i, j)),
        out_shape=jax.ShapeDtypeStruct((M, D), F32),
        scratch_shapes=[pltpu.VMEM((tm, A), BF16)],
        compiler_params=_params("arbitrary", "arbitrary"),
        name="out_proj",
    )(rec_n, att, gn_att.reshape(1, A), w_out, w_out, x, gate.reshape(B, 1, D))


def _gateup_kernel(h_ref, wg_ref, wu_ref, o_ref):
    h = h_ref[...]
    g = jnp.dot(h, wg_ref[...].astype(BF16), preferred_element_type=F32)
    u = jnp.dot(h, wu_ref[...].astype(BF16), preferred_element_type=F32)
    o_ref[...] = (g * jax.nn.sigmoid(g) * u).astype(o_ref.dtype)


def _gate_up(h, w_gate, w_up, tm=2048, tf=256):
    M, D = h.shape
    F = w_gate.shape[1]
    return pl.pallas_call(
        _gateup_kernel,
        grid=(M // tm, F // tf),
        in_specs=[pl.BlockSpec((tm, D), lambda i, j: (i, 0), pipeline_mode=pl.Buffered(1)),
                  pl.BlockSpec((D, tf), lambda i, j: (0, j)),
                  pl.BlockSpec((D, tf), lambda i, j: (0, j))],
        out_specs=pl.BlockSpec((tm, tf), lambda i, j: (i, j)),
        out_shape=jax.ShapeDtypeStruct((M, F), BF16),
        compiler_params=_params("arbitrary", "arbitrary"),
        name="gate_up",
    )(h, w_gate, w_up)


def _down_kernel(a_ref, w_ref, x_ref, g_ref, o_ref):
    acc = jnp.dot(a_ref[...], w_ref[...], preferred_element_type=F32)
    o_ref[...] = x_ref[...] + g_ref[...] * acc


def _down(act, w_down, x, gate, tm=512, tn=512):
    M, F = act.shape
    D = w_down.shape[1]
    B = gate.shape[0]
    S = M // B
    return pl.pallas_call(
        _down_kernel,
        grid=(M // tm, D // tn),
        in_specs=[pl.BlockSpec((tm, F), lambda i, j: (i, 0)),
                  pl.BlockSpec((F, tn), lambda i, j: (0, j)),
                  pl.BlockSpec((tm, tn), lambda i, j: (i, j)),
                  pl.BlockSpec((None, 1, tn), lambda i, j: (i * tm // S, 0, j))],
        out_specs=pl.BlockSpec((tm, tn), lambda i, j: (i, j)),
        out_shape=jax.ShapeDtypeStruct((M, D), F32),
        compiler_params=_params("arbitrary", "arbitrary"),
        name="down_proj",
    )(act, w_down, x, gate.reshape(B, 1, D))


def kernel(x, c, ada_w, ada_b, norm1_g, norm2_g, w_in, conv_w, conv_b, rg_w_a, rg_b_a, rg_w_i, rg_b_i,
           lru_lambda, rel_bias, gnorm_rec, gnorm_attn, w_out, w_gate, w_up, w_down, final_g):
    B, S, D = x.shape
    depth = ada_w.shape[0]
    R = conv_w.shape[-1]
    A = gnorm_attn.shape[-1]
    n_heads = A // HEAD_DIM
    M = B * S
    for l in range(depth):
        mod = _ada_mod(c, ada_w[l], ada_b[l])
        sh1, sc1, g1, sh2, sc2, g2 = jnp.split(mod, N_ADA, axis=-1)

        rec_in, qkv = _in_proj(x, norm1_g[l], sc1, sh1, w_in[l], 2 * R)
        rec_in = rec_in.reshape(B, S, rec_in.shape[-1])
        qkv = qkv.reshape(B, S, qkv.shape[-1])
        rec_n = _rglru(rec_in, conv_w[l], conv_b[l], rg_w_a[l], rg_b_a[l], rg_w_i[l], rg_b_i[l],
                       lru_lambda[l], gnorm_rec[l])
        att = _dilated_attention(qkv, rel_bias, n_heads, col0=INPROJ_TN)
        x = _out_proj(rec_n.reshape(M, R), att.reshape(M, A), gnorm_attn[l], w_out[l],
                      x.reshape(M, D), g1).reshape(B, S, D)

        h = _norm_mod(x, norm2_g[l], sc2, sh2, BF16).reshape(M, D)
        act = _gate_up(h, w_gate[l], w_up[l])
        x = _down(act, w_down[l].astype(BF16), x.reshape(M, D), g2).reshape(B, S, D)
    zeros = jnp.zeros((B, D), F32)
    return _norm_mod(x, final_g, zeros, zeros, x.dtype)
```
